```python
import math
import jax, jax.numpy as jnp
from jax import lax
import numpy as np

D_MODEL = 1024
BATCH = 8
SEQ = 4096
DEPTH = 2
DEC_BATCH = 32
DEC_SEQ = 1
PAST_LEN = 16384
PAGE_SIZE = 128

HEAD_DIM = 64
RWKV_WIDTH = D_MODEL // 2
MOBA_WIDTH = D_MODEL - RWKV_WIDTH
RWKV_HEADS = RWKV_WIDTH // HEAD_DIM
MOBA_HEADS = MOBA_WIDTH // HEAD_DIM
LORA_W = max(32, int(round(1.8 * D_MODEL ** 0.5 / 32)) * 32)
LORA_A = max(32, int(round(1.8 * D_MODEL ** 0.5 / 32)) * 32)
LORA_G = max(32, int(round(0.6 * D_MODEL ** 0.8 / 32)) * 32)
RWKV_PROJ = 3 * RWKV_WIDTH + LORA_W + LORA_A + LORA_G
MOBA_PROJ = 3 * MOBA_WIDTH
IN_PROJ = RWKV_PROJ + MOBA_PROJ
GN_EPS = 64e-5
MOBA_BLOCK = 256
MOBA_TOPK = 3
QUERY_BLOCK = 16
ROPE_THETA = 10000.0
NEG_INF = -1e30
MEM_LEN = 256
MEM_HEADS = 4
MEM_HEAD_DIM = D_MODEL // MEM_HEADS
PEER_HEADS = 8
PEER_NKEYS = 128
PEER_EXPERTS = PEER_NKEYS * PEER_NKEYS
PEER_TOPK = 16
PEER_DKEY = 256
PEER_HALF = PEER_DKEY // 2
PEER_CHUNK = 256
DEEPNORM_ALPHA = (2 * DEPTH) ** 0.25
DEEPNORM_BETA = (8 * DEPTH) ** -0.25
LN_EPS = 1e-5

kernel_name = 'hymba_rwkv7_moba_peer_step'


def layer_norm(x, g, b):
    xf = x.astype(jnp.float32)
    mu = jnp.mean(xf, -1, keepdims=True)
    var = jnp.mean(jnp.square(xf - mu), -1, keepdims=True)
    return ((xf - mu) * lax.rsqrt(var + LN_EPS)).astype(x.dtype) * g + b


def rope(x, pos):
    half = x.shape[-1] // 2
    inv = ROPE_THETA ** (-jnp.arange(half, dtype=jnp.float32) / half)
    ang = pos.astype(jnp.float32)[:, None] * inv[None, :]
    cos = jnp.cos(ang)[None, :, None, :]
    sin = jnp.sin(ang)[None, :, None, :]
    x1 = x[..., :half].astype(jnp.float32)
    x2 = x[..., half:].astype(jnp.float32)
    return jnp.concatenate([x1 * cos - x2 * sin, x2 * cos + x1 * sin], -1).astype(x.dtype)


def wkv_scan(r, decay, k, v, kk, a, s0):
    def step(s, inp):
        r_t, w_t, k_t, v_t, kk_t, a_t = inp
        sa = jnp.einsum('bhvk,bhk->bhv', s, -kk_t)
        s = (s * w_t[:, :, None, :] + sa[..., None] * (kk_t * a_t)[:, :, None, :]
             + v_t[..., None] * k_t[:, :, None, :])
        return s, jnp.einsum('bhvk,bhk->bhv', s, r_t)
    xs = tuple(jnp.moveaxis(t, 1, 0) for t in (r, decay, k, v, kk, a))
    s, ys = lax.scan(step, s0, xs)
    return jnp.moveaxis(ys, 0, 1), s


def rwkv_group(proj, prev, s0, mu, w0, w2, a0, a2, g2, k_k, k_a, r_k, lnx_g, lnx_b):
    B, T, _ = proj.shape
    shifted = jnp.concatenate([prev[:, None, :].astype(proj.dtype), proj[:, :-1]], axis=1)
    xm = proj + mu * (shifted - proj)
    c1, c2, c3 = RWKV_WIDTH, 2 * RWKV_WIDTH, 3 * RWKV_WIDTH
    c4 = c3 + LORA_W
    c5 = c4 + LORA_A
    r, k, v = xm[..., :c1], xm[..., c1:c2], xm[..., c2:c3]
    xw, xa, xg = xm[..., c3:c4], xm[..., c4:c5], xm[..., c5:]
    w = -jax.nn.softplus(-(w0 + jnp.tanh(xw) @ w2)) - 0.5
    decay = jnp.exp(-jnp.exp(w.astype(jnp.float32)))
    a = jax.nn.sigmoid(a0 + xa @ a2)
    g = jax.nn.sigmoid(xg) @ g2
    heads = lambda t: t.astype(jnp.float32).reshape(B, T, RWKV_HEADS, HEAD_DIM)
    kk = heads(k * k_k)
    kk = kk / jnp.maximum(jnp.sqrt(jnp.sum(kk * kk, -1, keepdims=True)), 1e-12)
    k = k * (1.0 + (a - 1.0) * k_a)
    r_h, k_h, v_h, a_h, w_h = heads(r), heads(k), heads(v), heads(a), heads(decay)
    y, s = wkv_scan(r_h, w_h, k_h, v_h, kk, a_h, s0.astype(jnp.float32))
    mean = jnp.mean(y, -1, keepdims=True)
    var = jnp.mean(jnp.square(y - mean), -1, keepdims=True)
    y = ((y - mean) * lax.rsqrt(var + GN_EPS)).reshape(B, T, RWKV_WIDTH) * lnx_g + lnx_b
    bonus = (jnp.sum(r_h * k_h * r_k, -1, keepdims=True) * v_h).reshape(B, T, RWKV_WIDTH)
    out = ((y + bonus) * g).astype(proj.dtype)
    return out, s.astype(s0.dtype), proj[:, -1]


def moba_attend(q, k_all, v_all, q_pos0):
    B, Tq, H, hd = q.shape
    L = k_all.shape[1]
    qc = QUERY_BLOCK if Tq >= QUERY_BLOCK else Tq
    n_chunks = -(-Tq // qc)
    tq_pad = n_chunks * qc
    n_blocks = -(-(q_pos0 + tq_pad) // MOBA_BLOCK)
    l_pad = n_blocks * MOBA_BLOCK
    k_sel = min(MOBA_TOPK, n_blocks)
    q = jnp.pad(q, ((0, 0), (0, tq_pad - Tq), (0, 0), (0, 0)))
    kb = jnp.pad(k_all, ((0, 0), (0, l_pad - L), (0, 0), (0, 0))).reshape(B, n_blocks, MOBA_BLOCK, H, hd)
    vb = jnp.pad(v_all, ((0, 0), (0, l_pad - L), (0, 0), (0, 0))).reshape(B, n_blocks, MOBA_BLOCK, H, hd)
    k_mean = jnp.mean(kb, axis=2, dtype=jnp.float32)
    qs = jnp.moveaxis(q.reshape(B, n_chunks, qc, H, hd), 1, 0)
    scale = hd ** -0.5
    b_idx = jnp.arange(B)[:, None, None, None]
    h_idx = jnp.arange(H)[None, :, None, None]
    blk_ids = jnp.arange(n_blocks)
    offs = jnp.arange(MOBA_BLOCK)

    def chunk_fn(args):
        c, qx = args
        start = q_pos0 + c * qc
        cur = start // MOBA_BLOCK
        pos = start + jnp.arange(qc)
        gate = jnp.einsum('bqhd,bnhd->bhqn', qx.astype(jnp.float32), k_mean)
        gate = jnp.where(blk_ids < cur, gate, NEG_INF)
        _, sel = lax.top_k(gate, k_sel)
        sel_ok = sel < cur
        ks = kb[b_idx, sel, :, h_idx]
        vs = vb[b_idx, sel, :, h_idx]
        s_sel = jnp.einsum('bqhd,bhqnjd->bhqnj', qx, ks, preferred_element_type=jnp.float32) * scale
        s_sel = jnp.where(sel_ok[..., None], s_sel, NEG_INF).reshape(B, H, qc, k_sel * MOBA_BLOCK)
        ko = lax.dynamic_index_in_dim(kb, cur, axis=1, keepdims=False)
        vo = lax.dynamic_index_in_dim(vb, cur, axis=1, keepdims=False)
        s_own = jnp.einsum('bqhd,bjhd->bhqj', qx, ko, preferred_element_type=jnp.float32) * scale
        s_own = jnp.where((cur * MOBA_BLOCK + offs)[None, :] <= pos[:, None], s_own, NEG_INF)
        p = jax.nn.softmax(jnp.concatenate([s_sel, s_own], -1), axis=-1).astype(qx.dtype)
        p_sel = p[..., :k_sel * MOBA_BLOCK].reshape(B, H, qc, k_sel, MOBA_BLOCK)
        p_own = p[..., k_sel * MOBA_BLOCK:]
        return (jnp.einsum('bhqnj,bhqnjd->bqhd', p_sel, vs)
                + jnp.einsum('bhqj,bjhd->bqhd', p_own, vo))

    out = lax.map(chunk_fn, (jnp.arange(n_chunks), qs))
    return jnp.moveaxis(out, 0, 1).reshape(B, tq_pad, H, hd)[:, :Tq]


def mem_attend(x, mk, mv, wq, wo):
    B, T, _ = x.shape
    q = (x @ wq).reshape(B, T, MEM_HEADS, MEM_HEAD_DIM)
    s = jnp.einsum('bthd,bmhd->bhtm', q, mk.astype(q.dtype), preferred_element_type=jnp.float32) * MEM_HEAD_DIM ** -0.5
    prob = jax.nn.softmax(s, axis=-1).astype(x.dtype)
    o = jnp.einsum('bhtm,bmhd->bthd', prob, mv.astype(x.dtype)).reshape(B, T, D_MODEL)
    return o @ wo


def peer_ffn(x, wq, sub_keys, u_tab, v_tab):
    B, T, D = x.shape
    n = B * T
    chunk = min(PEER_CHUNK, n)
    n_ch = -(-n // chunk)
    xp = jnp.pad(x.reshape(n, D), ((0, n_ch * chunk - n), (0, 0))).reshape(n_ch, chunk, D)

    def chunk_fn(xc):
        q = (xc @ wq).reshape(chunk, PEER_HEADS, 2, PEER_HALF)
        s1 = jnp.einsum('chd,kd->chk', q[:, :, 0], sub_keys[0], preferred_element_type=jnp.float32)
        s2 = jnp.einsum('chd,kd->chk', q[:, :, 1], sub_keys[1], preferred_element_type=jnp.float32)
        v1, i1 = lax.top_k(s1, PEER_TOPK)
        v2, i2 = lax.top_k(s2, PEER_TOPK)
        cand = (v1[..., :, None] + v2[..., None, :]).reshape(chunk, PEER_HEADS, PEER_TOPK * PEER_TOPK)
        cand_idx = (i1[..., :, None] * PEER_NKEYS + i2[..., None, :]).reshape(chunk, PEER_HEADS, PEER_TOPK * PEER_TOPK)
        top_s, top_pos = lax.top_k(cand, PEER_TOPK)
        e = jnp.take_along_axis(cand_idx, top_pos, axis=-1)
        g = jax.nn.softmax(top_s, axis=-1)
        hid = jax.nn.gelu(jnp.einsum('cd,ched->che', xc, u_tab[e]), approximate=False)
        wgt = (g * hid).astype(xc.dtype)
        return jnp.einsum('che,ched->cd', wgt, v_tab[e])

    out = lax.map(chunk_fn, xp).reshape(n_ch * chunk, D)[:n]
    return out.reshape(B, T, D)


def gather_pages(pool, page_table):
    g = pool[page_table]
    return g.reshape(g.shape[0], g.shape[1] * g.shape[2], g.shape[3], g.shape[4])


def run_trunk(x, pos0, shift0, wkv0, cache_k, cache_v, page_table, mem_k, mem_v, p):
    B, T, _ = x.shape
    pos = pos0 + jnp.arange(T, dtype=jnp.int32)
    k_rows, v_rows, wkvs, shifts = [], [], [], []
    for i in range(DEPTH):
        proj = x @ p['w_in'][i]
        rw_out, wkv_i, shift_i = rwkv_group(
            proj[..., :RWKV_PROJ], shift0[i], wkv0[i], p['shift_mu'][i],
            p['rwkv_w0'][i], p['rwkv_w2'][i], p['rwkv_a0'][i], p['rwkv_a2'][i],
            p['rwkv_g2'][i], p['rwkv_k_k'][i], p['rwkv_k_a'][i], p['rwkv_r_k'][i],
            p['rwkv_lnx_g'][i], p['rwkv_lnx_b'][i])
        q, k, v = jnp.split(proj[..., RWKV_PROJ:], 3, axis=-1)
        q = rope(q.reshape(B, T, MOBA_HEADS, HEAD_DIM), pos)
        k = rope(k.reshape(B, T, MOBA_HEADS, HEAD_DIM), pos)
        v = v.reshape(B, T, MOBA_HEADS, HEAD_DIM)
        if cache_k is None:
            k_all, v_all = k, v
        else:
            k_all = jnp.concatenate([gather_pages(cache_k[i], page_table).astype(k.dtype), k], axis=1)
            v_all = jnp.concatenate([gather_pages(cache_v[i], page_table).astype(v.dtype), v], axis=1)
        mb_out = moba_attend(q, k_all, v_all, pos0).reshape(B, T, MOBA_WIDTH)
        mix = jnp.concatenate([rw_out, mb_out], axis=-1) @ p['w_out'][i]
        x = layer_norm(DEEPNORM_ALPHA * x + mix, p['ln1_g'][i], p['ln1_b'][i])
        cross = mem_attend(x, mem_k[i], mem_v[i], p['w_mem_q'][i], p['w_mem_o'][i])
        x = layer_norm(DEEPNORM_ALPHA * x + cross, p['ln2_g'][i], p['ln2_b'][i])
        ffn = peer_ffn(x, p['peer_w_q'][i], p['peer_sub_keys'][i], p['peer_u'][i], p['peer_v'][i])
        x = layer_norm(DEEPNORM_ALPHA * x + ffn, p['ln3_g'][i], p['ln3_b'][i])
        k_rows.append(k)
        v_rows.append(v)
        wkvs.append(wkv_i)
        shifts.append(shift_i)
    return x, jnp.stack(k_rows), jnp.stack(v_rows), jnp.stack(wkvs), jnp.stack(shifts)


def setup_inputs(seed: int = 0) -> dict:
    key = jax.random.key(seed)
    keys = jax.random.split(key, 40)
    nrm = lambda kk, shape, scale: jax.random.normal(kk, shape, jnp.float32) * scale
    n_pages = PAST_LEN // PAGE_SIZE
    n_used = DEC_BATCH * n_pages
    n_pool = n_used + max(1, n_used // 4)
    page_table = jax.random.permutation(keys[5], n_pool)[:n_used].reshape(DEC_BATCH, n_pages).astype(jnp.int32)
    dm = D_MODEL ** -0.5
    return {
        'x_prompt': nrm(keys[0], (BATCH, SEQ, D_MODEL), 1.0),
        'x_sample': nrm(keys[1], (DEC_BATCH, DEC_SEQ, D_MODEL), 1.0),
        'mem_prompt': nrm(keys[2], (BATCH, MEM_LEN, D_MODEL), 1.0),
        'cache_k': nrm(keys[3], (DEPTH, n_pool, PAGE_SIZE, MOBA_HEADS, HEAD_DIM), 1.0),
        'cache_v': nrm(keys[4], (DEPTH, n_pool, PAGE_SIZE, MOBA_HEADS, HEAD_DIM), 1.0),
        'page_table': page_table,
        'state_wkv': nrm(keys[6], (DEPTH, DEC_BATCH, RWKV_HEADS, HEAD_DIM, HEAD_DIM), 0.3),
        'state_shift': nrm(keys[7], (DEPTH, DEC_BATCH, RWKV_PROJ), 1.0),
        'cache_mem_k': nrm(keys[8], (DEPTH, DEC_BATCH, MEM_LEN, MEM_HEADS, MEM_HEAD_DIM), 1.0),
        'cache_mem_v': nrm(keys[9], (DEPTH, DEC_BATCH, MEM_LEN, MEM_HEADS, MEM_HEAD_DIM), 1.0),
        'w_in': nrm(keys[10], (DEPTH, D_MODEL, IN_PROJ), dm),
        'shift_mu': jax.random.uniform(keys[11], (DEPTH, RWKV_PROJ), jnp.float32),
        'rwkv_w0': nrm(keys[12], (DEPTH, RWKV_WIDTH), 0.5) - 1.0,
        'rwkv_w2': nrm(keys[13], (DEPTH, LORA_W, RWKV_WIDTH), 0.5 * LORA_W ** -0.5),
        'rwkv_a0': nrm(keys[14], (DEPTH, RWKV_WIDTH), 0.5),
        'rwkv_a2': nrm(keys[15], (DEPTH, LORA_A, RWKV_WIDTH), 0.5 * LORA_A ** -0.5),
        'rwkv_g2': nrm(keys[16], (DEPTH, LORA_G, RWKV_WIDTH), LORA_G ** -0.5),
        'rwkv_k_k': 1.0 + nrm(keys[17], (DEPTH, RWKV_WIDTH), 0.1),
        'rwkv_k_a': 1.0 + nrm(keys[18], (DEPTH, RWKV_WIDTH), 0.1),
        'rwkv_r_k': nrm(keys[19], (DEPTH, RWKV_HEADS, HEAD_DIM), 0.1),
        'rwkv_lnx_g': 1.0 + nrm(keys[20], (DEPTH, RWKV_WIDTH), 0.01),
        'rwkv_lnx_b': nrm(keys[21], (DEPTH, RWKV_WIDTH), 0.01),
        'w_out': nrm(keys[22], (DEPTH, D_MODEL, D_MODEL), dm * DEEPNORM_BETA),
        'ln1_g': 1.0 + nrm(keys[23], (DEPTH, D_MODEL), 0.01),
        'ln1_b': nrm(keys[24], (DEPTH, D_MODEL), 0.01),
        'w_mem_q': nrm(keys[25], (DEPTH, D_MODEL, D_MODEL), dm),
        'w_mem_kv': nrm(keys[26], (DEPTH, D_MODEL, 2 * D_MODEL), dm),
        'w_mem_o': nrm(keys[27], (DEPTH, D_MODEL, D_MODEL), dm * DEEPNORM_BETA),
        'ln2_g': 1.0 + nrm(keys[28], (DEPTH, D_MODEL), 0.01),
        'ln2_b': nrm(keys[29], (DEPTH, D_MODEL), 0.01),
        'peer_w_q': nrm(keys[30], (DEPTH, D_MODEL, PEER_HEADS * PEER_DKEY), dm),
        'peer_sub_keys': nrm(keys[31], (DEPTH, 2, PEER_NKEYS, PEER_HALF), PEER_HALF ** -0.5),
        'peer_u': nrm(keys[32], (DEPTH, PEER_EXPERTS, D_MODEL), dm),
        'peer_v': nrm(keys[33], (DEPTH, PEER_EXPERTS, D_MODEL), DEEPNORM_BETA * PEER_HEADS ** -0.5),
        'ln3_g': 1.0 + nrm(keys[34], (DEPTH, D_MODEL), 0.01),
        'ln3_b': nrm(keys[35], (DEPTH, D_MODEL), 0.01),
    }


def reference(x_prompt, x_sample, mem_prompt, cache_k, cache_v, page_table, state_wkv,
              state_shift, cache_mem_k, cache_mem_v, w_in, shift_mu, rwkv_w0, rwkv_w2,
              rwkv_a0, rwkv_a2, rwkv_g2, rwkv_k_k, rwkv_k_a, rwkv_r_k, rwkv_lnx_g,
              rwkv_lnx_b, w_out, ln1_g, ln1_b, w_mem_q, w_mem_kv, w_mem_o, ln2_g, ln2_b,
              peer_w_q, peer_sub_keys, peer_u, peer_v, ln3_g, ln3_b):
    p = dict(w_in=w_in, shift_mu=shift_mu, rwkv_w0=rwkv_w0, rwkv_w2=rwkv_w2,
             rwkv_a0=rwkv_a0, rwkv_a2=rwkv_a2, rwkv_g2=rwkv_g2, rwkv_k_k=rwkv_k_k,
             rwkv_k_a=rwkv_k_a, rwkv_r_k=rwkv_r_k, rwkv_lnx_g=rwkv_lnx_g,
             rwkv_lnx_b=rwkv_lnx_b, w_out=w_out, ln1_g=ln1_g, ln1_b=ln1_b,
             w_mem_q=w_mem_q, w_mem_o=w_mem_o, ln2_g=ln2_g, ln2_b=ln2_b,
             peer_w_q=peer_w_q, peer_sub_keys=peer_sub_keys, peer_u=peer_u,
             peer_v=peer_v, ln3_g=ln3_g, ln3_b=ln3_b)
    B = x_prompt.shape[0]
    mkv = jnp.einsum('bmd,lde->lbme', mem_prompt, w_mem_kv)
    mem_k_prompt = mkv[..., :D_MODEL].reshape(DEPTH, B, MEM_LEN, MEM_HEADS, MEM_HEAD_DIM)
    mem_v_prompt = mkv[..., D_MODEL:].reshape(DEPTH, B, MEM_LEN, MEM_HEADS, MEM_HEAD_DIM)
    shift0 = jnp.zeros((DEPTH, B, RWKV_PROJ), x_prompt.dtype)
    wkv0 = jnp.zeros((DEPTH, B, RWKV_HEADS, HEAD_DIM, HEAD_DIM), x_prompt.dtype)
    y_prompt, k_prompt, v_prompt, wkv_prompt, shift_prompt = run_trunk(
        x_prompt, 0, shift0, wkv0, None, None, None, mem_k_prompt, mem_v_prompt, p)
    past_len = page_table.shape[1] * PAGE_SIZE
    y_sample, k_sample, v_sample, wkv_sample, shift_sample = run_trunk(
        x_sample, past_len, state_shift, state_wkv, cache_k, cache_v, page_table,
        cache_mem_k, cache_mem_v, p)
    return (y_prompt, y_sample, k_prompt, v_prompt, wkv_prompt, shift_prompt,
            mem_k_prompt, mem_v_prompt, k_sample, v_sample, wkv_sample, shift_sample)
```

```python
import functools
import math

import jax
import jax.numpy as jnp
from jax import lax
from jax.experimental import pallas as pl
from jax.experimental.pallas import tpu as pltpu

D_MODEL = 1024
DEPTH = 2
PAGE_SIZE = 128
HEAD_DIM = 64
RWKV_WIDTH = D_MODEL // 2
MOBA_WIDTH = D_MODEL - RWKV_WIDTH
RWKV_HEADS = RWKV_WIDTH // HEAD_DIM
MOBA_HEADS = MOBA_WIDTH // HEAD_DIM
LORA_W = max(32, int(round(1.8 * D_MODEL ** 0.5 / 32)) * 32)
LORA_A = max(32, int(round(1.8 * D_MODEL ** 0.5 / 32)) * 32)
LORA_G = max(32, int(round(0.6 * D_MODEL ** 0.8 / 32)) * 32)
RWKV_PROJ = 3 * RWKV_WIDTH + LORA_W + LORA_A + LORA_G
MOBA_PROJ = 3 * MOBA_WIDTH
IN_PROJ = RWKV_PROJ + MOBA_PROJ
GN_EPS = 64e-5
MOBA_BLOCK = 256
MOBA_TOPK = 3
QUERY_BLOCK = 16
ROPE_THETA = 10000.0
NEG_INF = -1e30
MEM_HEADS = 4
MEM_HEAD_DIM = D_MODEL // MEM_HEADS
PEER_HEADS = 8
PEER_NKEYS = 128
PEER_TOPK = 16
PEER_DKEY = 256
PEER_HALF = PEER_DKEY // 2
PEER_CHUNK = 256
DEEPNORM_ALPHA = (2 * DEPTH) ** 0.25
LN_EPS = 1e-5

VMEM_LIMIT_BYTES = 56 * 1024 * 1024


def _matmul_kernel(x_ref, w_ref, o_ref):
    o_ref[...] = jnp.dot(x_ref[...].astype(jnp.bfloat16), w_ref[...].astype(jnp.bfloat16),
                         preferred_element_type=jnp.float32)


def pmatmul(x, w, tm=512, tn=512):
    m, k = x.shape
    n = w.shape[1]
    tm = min(tm, m)
    assert m % tm == 0
    if n % tn != 0:
        tn = n
    return pl.pallas_call(
        _matmul_kernel,
        grid=(m // tm, n // tn),
        in_specs=[pl.BlockSpec((tm, k), lambda i, j: (i, 0)),
                  pl.BlockSpec((k, tn), lambda i, j: (0, j))],
        out_specs=pl.BlockSpec((tm, tn), lambda i, j: (i, j)),
        out_shape=jax.ShapeDtypeStruct((m, n), jnp.float32),
        compiler_params=pltpu.CompilerParams(
            dimension_semantics=("parallel", "parallel"),
            vmem_limit_bytes=VMEM_LIMIT_BYTES),
    )(x, w)


def layer_norm(x, g, b):
    xf = x.astype(jnp.float32)
    mu = jnp.mean(xf, -1, keepdims=True)
    var = jnp.mean(jnp.square(xf - mu), -1, keepdims=True)
    return ((xf - mu) * lax.rsqrt(var + LN_EPS)).astype(x.dtype) * g + b


def rope(x, pos):
    half = x.shape[-1] // 2
    inv = ROPE_THETA ** (-jnp.arange(half, dtype=jnp.float32) / half)
    ang = pos.astype(jnp.float32)[:, None] * inv[None, :]
    cos = jnp.cos(ang)[None, :, None, :]
    sin = jnp.sin(ang)[None, :, None, :]
    x1 = x[..., :half].astype(jnp.float32)
    x2 = x[..., half:].astype(jnp.float32)
    return jnp.concatenate([x1 * cos - x2 * sin, x2 * cos + x1 * sin], -1).astype(x.dtype)


def wkv_scan(r, decay, k, v, kk, a, s0):
    def step(s, inp):
        r_t, w_t, k_t, v_t, kk_t, a_t = inp
        sa = jnp.einsum('bhvk,bhk->bhv', s, -kk_t)
        s = (s * w_t[:, :, None, :] + sa[..., None] * (kk_t * a_t)[:, :, None, :]
             + v_t[..., None] * k_t[:, :, None, :])
        return s, jnp.einsum('bhvk,bhk->bhv', s, r_t)
    xs = tuple(jnp.moveaxis(t, 1, 0) for t in (r, decay, k, v, kk, a))
    s, ys = lax.scan(step, s0, xs)
    return jnp.moveaxis(ys, 0, 1), s


def rwkv_group(proj, prev, s0, mu, w0, w2, a0, a2, g2, k_k, k_a, r_k, lnx_g, lnx_b):
    B, T, _ = proj.shape
    shifted = jnp.concatenate([prev[:, None, :].astype(proj.dtype), proj[:, :-1]], axis=1)
    xm = proj + mu * (shifted - proj)
    c1, c2, c3 = RWKV_WIDTH, 2 * RWKV_WIDTH, 3 * RWKV_WIDTH
    c4 = c3 + LORA_W
    c5 = c4 + LORA_A
    r, k, v = xm[..., :c1], xm[..., c1:c2], xm[..., c2:c3]
    xw, xa, xg = xm[..., c3:c4], xm[..., c4:c5], xm[..., c5:]
    w = -jax.nn.softplus(-(w0 + jnp.tanh(xw) @ w2)) - 0.5
    decay = jnp.exp(-jnp.exp(w.astype(jnp.float32)))
    a = jax.nn.sigmoid(a0 + xa @ a2)
    g = jax.nn.sigmoid(xg) @ g2
    heads = lambda t: t.astype(jnp.float32).reshape(B, T, RWKV_HEADS, HEAD_DIM)
    kk = heads(k * k_k)
    kk = kk / jnp.maximum(jnp.sqrt(jnp.sum(kk * kk, -1, keepdims=True)), 1e-12)
    k = k * (1.0 + (a - 1.0) * k_a)
    r_h, k_h, v_h, a_h, w_h = heads(r), heads(k), heads(v), heads(a), heads(decay)
    y, s = wkv_scan(r_h, w_h, k_h, v_h, kk, a_h, s0.astype(jnp.float32))
    mean = jnp.mean(y, -1, keepdims=True)
    var = jnp.mean(jnp.square(y - mean), -1, keepdims=True)
    y = ((y - mean) * lax.rsqrt(var + GN_EPS)).reshape(B, T, RWKV_WIDTH) * lnx_g + lnx_b
    bonus = (jnp.sum(r_h * k_h * r_k, -1, keepdims=True) * v_h).reshape(B, T, RWKV_WIDTH)
    out = ((y + bonus) * g).astype(proj.dtype)
    return out, s.astype(s0.dtype), proj[:, -1]


def moba_attend(q, k_all, v_all, q_pos0):
    B, Tq, H, hd = q.shape
    L = k_all.shape[1]
    qc = QUERY_BLOCK if Tq >= QUERY_BLOCK else Tq
    n_chunks = -(-Tq // qc)
    tq_pad = n_chunks * qc
    n_blocks = -(-(q_pos0 + tq_pad) // MOBA_BLOCK)
    l_pad = n_blocks * MOBA_BLOCK
    k_sel = min(MOBA_TOPK, n_blocks)
    q = jnp.pad(q, ((0, 0), (0, tq_pad - Tq), (0, 0), (0, 0)))
    kb = jnp.pad(k_all, ((0, 0), (0, l_pad - L), (0, 0), (0, 0))).reshape(B, n_blocks, MOBA_BLOCK, H, hd)
    vb = jnp.pad(v_all, ((0, 0), (0, l_pad - L), (0, 0), (0, 0))).reshape(B, n_blocks, MOBA_BLOCK, H, hd)
    k_mean = jnp.mean(kb, axis=2, dtype=jnp.float32)
    qs = jnp.moveaxis(q.reshape(B, n_chunks, qc, H, hd), 1, 0)
    scale = hd ** -0.5
    b_idx = jnp.arange(B)[:, None, None, None]
    h_idx = jnp.arange(H)[None, :, None, None]
    blk_ids = jnp.arange(n_blocks)
    offs = jnp.arange(MOBA_BLOCK)

    def chunk_fn(args):
        c, qx = args
        start = q_pos0 + c * qc
        cur = start // MOBA_BLOCK
        pos = start + jnp.arange(qc)
        gate = jnp.einsum('bqhd,bnhd->bhqn', qx.astype(jnp.float32), k_mean)
        gate = jnp.where(blk_ids < cur, gate, NEG_INF)
        _, sel = lax.top_k(gate, k_sel)
        sel_ok = sel < cur
        ks = kb[b_idx, sel, :, h_idx]
        vs = vb[b_idx, sel, :, h_idx]
        s_sel = jnp.einsum('bqhd,bhqnjd->bhqnj', qx, ks, preferred_element_type=jnp.float32) * scale
        s_sel = jnp.where(sel_ok[..., None], s_sel, NEG_INF).reshape(B, H, qc, k_sel * MOBA_BLOCK)
        ko = lax.dynamic_index_in_dim(kb, cur, axis=1, keepdims=False)
        vo = lax.dynamic_index_in_dim(vb, cur, axis=1, keepdims=False)
        s_own = jnp.einsum('bqhd,bjhd->bhqj', qx, ko, preferred_element_type=jnp.float32) * scale
        s_own = jnp.where((cur * MOBA_BLOCK + offs)[None, :] <= pos[:, None], s_own, NEG_INF)
        p = jax.nn.softmax(jnp.concatenate([s_sel, s_own], -1), axis=-1).astype(qx.dtype)
        p_sel = p[..., :k_sel * MOBA_BLOCK].reshape(B, H, qc, k_sel, MOBA_BLOCK)
        p_own = p[..., k_sel * MOBA_BLOCK:]
        return (jnp.einsum('bhqnj,bhqnjd->bqhd', p_sel, vs)
                + jnp.einsum('bhqj,bjhd->bqhd', p_own, vo))

    out = lax.map(chunk_fn, (jnp.arange(n_chunks), qs))
    return jnp.moveaxis(out, 0, 1).reshape(B, tq_pad, H, hd)[:, :Tq]


def mem_attend(x, mk, mv, wq, wo):
    B, T, _ = x.shape
    q = pmatmul(x.reshape(B * T, D_MODEL), wq).reshape(B, T, MEM_HEADS, MEM_HEAD_DIM)
    s = jnp.einsum('bthd,bmhd->bhtm', q, mk.astype(q.dtype), preferred_element_type=jnp.float32) * MEM_HEAD_DIM ** -0.5
    prob = jax.nn.softmax(s, axis=-1).astype(x.dtype)
    o = jnp.einsum('bhtm,bmhd->bthd', prob, mv.astype(x.dtype)).reshape(B * T, D_MODEL)
    return pmatmul(o, wo).reshape(B, T, D_MODEL)


def peer_ffn(x, wq, sub_keys, u_tab, v_tab):
    B, T, D = x.shape
    n = B * T
    chunk = min(PEER_CHUNK, n)
    n_ch = -(-n // chunk)
    xp = jnp.pad(x.reshape(n, D), ((0, n_ch * chunk - n), (0, 0))).reshape(n_ch, chunk, D)

    def chunk_fn(xc):
        q = (xc @ wq).reshape(chunk, PEER_HEADS, 2, PEER_HALF)
        s1 = jnp.einsum('chd,kd->chk', q[:, :, 0], sub_keys[0], preferred_element_type=jnp.float32)
        s2 = jnp.einsum('chd,kd->chk', q[:, :, 1], sub_keys[1], preferred_element_type=jnp.float32)
        v1, i1 = lax.top_k(s1, PEER_TOPK)
        v2, i2 = lax.top_k(s2, PEER_TOPK)
        cand = (v1[..., :, None] + v2[..., None, :]).reshape(chunk, PEER_HEADS, PEER_TOPK * PEER_TOPK)
        cand_idx = (i1[..., :, None] * PEER_NKEYS + i2[..., None, :]).reshape(chunk, PEER_HEADS, PEER_TOPK * PEER_TOPK)
        top_s, top_pos = lax.top_k(cand, PEER_TOPK)
        e = jnp.take_along_axis(cand_idx, top_pos, axis=-1)
        g = jax.nn.softmax(top_s, axis=-1)
        hid = jax.nn.gelu(jnp.einsum('cd,ched->che', xc, u_tab[e]), approximate=False)
        wgt = (g * hid).astype(xc.dtype)
        return jnp.einsum('che,ched->cd', wgt, v_tab[e])

    out = lax.map(chunk_fn, xp).reshape(n_ch * chunk, D)[:n]
    return out.reshape(B, T, D)


def gather_pages(pool, page_table):
    g = pool[page_table]
    return g.reshape(g.shape[0], g.shape[1] * g.shape[2], g.shape[3], g.shape[4])


def run_trunk(x, pos0, shift0, wkv0, cache_k, cache_v, page_table, mem_k, mem_v, p):
    B, T, _ = x.shape
    pos = pos0 + jnp.arange(T, dtype=jnp.int32)
    k_rows, v_rows, wkvs, shifts = [], [], [], []
    for i in range(DEPTH):
        proj = pmatmul(x.reshape(B * T, D_MODEL), p['w_in'][i]).reshape(B, T, IN_PROJ)
        rw_out, wkv_i, shift_i = rwkv_group(
            proj[..., :RWKV_PROJ], shift0[i], wkv0[i], p['shift_mu'][i],
            p['rwkv_w0'][i], p['rwkv_w2'][i], p['rwkv_a0'][i], p['rwkv_a2'][i],
            p['rwkv_g2'][i], p['rwkv_k_k'][i], p['rwkv_k_a'][i], p['rwkv_r_k'][i],
            p['rwkv_lnx_g'][i], p['rwkv_lnx_b'][i])
        q, k, v = jnp.split(proj[..., RWKV_PROJ:], 3, axis=-1)
        q = rope(q.reshape(B, T, MOBA_HEADS, HEAD_DIM), pos)
        k = rope(k.reshape(B, T, MOBA_HEADS, HEAD_DIM), pos)
        v = v.reshape(B, T, MOBA_HEADS, HEAD_DIM)
        if cache_k is None:
            k_all, v_all = k, v
        else:
            k_all = jnp.concatenate([gather_pages(cache_k[i], page_table).astype(k.dtype), k], axis=1)
            v_all = jnp.concatenate([gather_pages(cache_v[i], page_table).astype(v.dtype), v], axis=1)
        mb_out = moba_attend(q, k_all, v_all, pos0).reshape(B, T, MOBA_WIDTH)
        mix = pmatmul(jnp.concatenate([rw_out, mb_out], axis=-1).reshape(B * T, D_MODEL),
                      p['w_out'][i]).reshape(B, T, D_MODEL)
        x = layer_norm(DEEPNORM_ALPHA * x + mix, p['ln1_g'][i], p['ln1_b'][i])
        cross = mem_attend(x, mem_k[i], mem_v[i], p['w_mem_q'][i], p['w_mem_o'][i])
        x = layer_norm(DEEPNORM_ALPHA * x + cross, p['ln2_g'][i], p['ln2_b'][i])
        ffn = peer_ffn(x, p['peer_w_q'][i], p['peer_sub_keys'][i], p['peer_u'][i], p['peer_v'][i])
        x = layer_norm(DEEPNORM_ALPHA * x + ffn, p['ln3_g'][i], p['ln3_b'][i])
        k_rows.append(k)
        v_rows.append(v)
        wkvs.append(wkv_i)
        shifts.append(shift_i)
    return x, jnp.stack(k_rows), jnp.stack(v_rows), jnp.stack(wkvs), jnp.stack(shifts)


def kernel(x_prompt, x_sample, mem_prompt, cache_k, cache_v, page_table, state_wkv,
           state_shift, cache_mem_k, cache_mem_v, w_in, shift_mu, rwkv_w0, rwkv_w2,
           rwkv_a0, rwkv_a2, rwkv_g2, rwkv_k_k, rwkv_k_a, rwkv_r_k, rwkv_lnx_g,
           rwkv_lnx_b, w_out, ln1_g, ln1_b, w_mem_q, w_mem_kv, w_mem_o, ln2_g, ln2_b,
           peer_w_q, peer_sub_keys, peer_u, peer_v, ln3_g, ln3_b):
    p = dict(w_in=w_in, shift_mu=shift_mu, rwkv_w0=rwkv_w0, rwkv_w2=rwkv_w2,
             rwkv_a0=rwkv_a0, rwkv_a2=rwkv_a2, rwkv_g2=rwkv_g2, rwkv_k_k=rwkv_k_k,
             rwkv_k_a=rwkv_k_a, rwkv_r_k=rwkv_r_k, rwkv_lnx_g=rwkv_lnx_g,
             rwkv_lnx_b=rwkv_lnx_b, w_out=w_out, ln1_g=ln1_g, ln1_b=ln1_b,
             w_mem_q=w_mem_q, w_mem_o=w_mem_o, ln2_g=ln2_g, ln2_b=ln2_b,
             peer_w_q=peer_w_q, peer_sub_keys=peer_sub_keys, peer_u=peer_u,
             peer_v=peer_v, ln3_g=ln3_g, ln3_b=ln3_b)
    B = x_prompt.shape[0]
    mem_len = mem_prompt.shape[1]
    mkv = jnp.stack([pmatmul(mem_prompt.reshape(B * mem_len, D_MODEL), w_mem_kv[i]).reshape(B, mem_len, 2 * D_MODEL)
                     for i in range(DEPTH)])
    mem_k_prompt = mkv[..., :D_MODEL].reshape(DEPTH, B, mem_len, MEM_HEADS, MEM_HEAD_DIM)
    mem_v_prompt = mkv[..., D_MODEL:].reshape(DEPTH, B, mem_len, MEM_HEADS, MEM_HEAD_DIM)
    shift0 = jnp.zeros((DEPTH, B, RWKV_PROJ), x_prompt.dtype)
    wkv0 = jnp.zeros((DEPTH, B, RWKV_HEADS, HEAD_DIM, HEAD_DIM), x_prompt.dtype)
    y_prompt, k_prompt, v_prompt, wkv_prompt, shift_prompt = run_trunk(
        x_prompt, 0, shift0, wkv0, None, None, None, mem_k_prompt, mem_v_prompt, p)
    past_len = page_table.shape[1] * PAGE_SIZE
    y_sample, k_sample, v_sample, wkv_sample, shift_sample = run_trunk(
        x_sample, past_len, state_shift, state_wkv, cache_k, cache_v, page_table,
        cache_mem_k, cache_mem_v, p)
    return (y_prompt, y_sample, k_prompt, v_prompt, wkv_prompt, shift_prompt,
            mem_k_prompt, mem_v_prompt, k_sample, v_sample, wkv_sample, shift_sample)
```

```python
import functools
import math

import jax
import jax.numpy as jnp
from jax import lax
from jax.experimental import pallas as pl
from jax.experimental.pallas import tpu as pltpu

D_MODEL = 1024
DEPTH = 2
PAGE_SIZE = 128
HEAD_DIM = 64
RWKV_WIDTH = D_MODEL // 2
MOBA_WIDTH = D_MODEL - RWKV_WIDTH
RWKV_HEADS = RWKV_WIDTH // HEAD_DIM
MOBA_HEADS = MOBA_WIDTH // HEAD_DIM
LORA_W = max(32, int(round(1.8 * D_MODEL ** 0.5 / 32)) * 32)
LORA_A = max(32, int(round(1.8 * D_MODEL ** 0.5 / 32)) * 32)
LORA_G = max(32, int(round(0.6 * D_MODEL ** 0.8 / 32)) * 32)
RWKV_PROJ = 3 * RWKV_WIDTH + LORA_W + LORA_A + LORA_G
MOBA_PROJ = 3 * MOBA_WIDTH
IN_PROJ = RWKV_PROJ + MOBA_PROJ
GN_EPS = 64e-5
MOBA_BLOCK = 256
MOBA_TOPK = 3
QUERY_BLOCK = 16
ROPE_THETA = 10000.0
NEG_INF = -1e30
MEM_HEADS = 4
MEM_HEAD_DIM = D_MODEL // MEM_HEADS
PEER_HEADS = 8
PEER_NKEYS = 128
PEER_TOPK = 16
PEER_DKEY = 256
PEER_HALF = PEER_DKEY // 2
PEER_CHUNK = 256
DEEPNORM_ALPHA = (2 * DEPTH) ** 0.25
LN_EPS = 1e-5

VMEM_LIMIT_BYTES = 56 * 1024 * 1024
SUBLANES = 8
LANES = 128


def _matmul_kernel(x_ref, w_ref, o_ref):
    o_ref[...] = jnp.dot(x_ref[...].astype(jnp.bfloat16), w_ref[...].astype(jnp.bfloat16),
                         preferred_element_type=jnp.float32)


def pmatmul(x, w, tm=512, tn=512):
    m, k = x.shape
    n = w.shape[1]
    tm = min(tm, m)
    assert m % tm == 0
    if n % tn != 0:
        tn = n
    return pl.pallas_call(
        _matmul_kernel,
        grid=(m // tm, n // tn),
        in_specs=[pl.BlockSpec((tm, k), lambda i, j: (i, 0)),
                  pl.BlockSpec((k, tn), lambda i, j: (0, j))],
        out_specs=pl.BlockSpec((tm, tn), lambda i, j: (i, j)),
        out_shape=jax.ShapeDtypeStruct((m, n), jnp.float32),
        compiler_params=pltpu.CompilerParams(
            dimension_semantics=("parallel", "parallel"),
            vmem_limit_bytes=VMEM_LIMIT_BYTES),
    )(x, w)


PEER_ROUTE_TOKENS = 256
PEER_GATHER_TOKENS = 8
PEER_SLOTS = PEER_HEADS * PEER_TOPK


def _topk_sublane(s, k):
    n_rows = s.shape[0]
    iota = lax.broadcasted_iota(jnp.int32, s.shape, 0)
    vals, idxs = [], []
    for _ in range(k):
        m = jnp.max(s, axis=0, keepdims=True)
        idx = jnp.min(jnp.where(s == m, iota, n_rows), axis=0, keepdims=True)
        vals.append(m)
        idxs.append(idx)
        s = jnp.where(iota == idx, -jnp.inf, s)
    return jnp.concatenate(vals, axis=0), jnp.concatenate(idxs, axis=0)


def _peer_route_kernel(x_ref, wq_ref, sk_ref, e_ref, g_ref):
    q = jnp.dot(x_ref[...].astype(jnp.bfloat16), wq_ref[...], preferred_element_type=jnp.float32)
    nt = (((1,), (1,)), ((), ()))
    for h in range(PEER_HEADS):
        base = h * PEER_DKEY
        q1 = q[:, base:base + PEER_HALF].astype(jnp.bfloat16)
        q2 = q[:, base + PEER_HALF:base + PEER_DKEY].astype(jnp.bfloat16)
        s1 = lax.dot_general(sk_ref[0], q1, nt, preferred_element_type=jnp.float32)
        s2 = lax.dot_general(sk_ref[1], q2, nt, preferred_element_type=jnp.float32)
        v1, i1 = _topk_sublane(s1, PEER_TOPK)
        v2, i2 = _topk_sublane(s2, PEER_TOPK)
        cand = jnp.concatenate([v1[a:a + 1, :] + v2 for a in range(PEER_TOPK)], axis=0)
        cand_idx = jnp.concatenate([i1[a:a + 1, :] * PEER_NKEYS + i2 for a in range(PEER_TOPK)], axis=0)
        n_cand = PEER_TOPK * PEER_TOPK
        iota = lax.broadcasted_iota(jnp.int32, cand.shape, 0)
        top_s, top_e = [], []
        for _ in range(PEER_TOPK):
            m = jnp.max(cand, axis=0, keepdims=True)
            pos = jnp.min(jnp.where(cand == m, iota, n_cand), axis=0, keepdims=True)
            hit = iota == pos
            top_s.append(m)
            top_e.append(jnp.sum(jnp.where(hit, cand_idx, 0), axis=0, keepdims=True))
            cand = jnp.where(hit, -jnp.inf, cand)
        top_s = jnp.concatenate(top_s, axis=0)
        p = jnp.exp(top_s - top_s[0:1, :])
        g = p / jnp.sum(p, axis=0, keepdims=True)
        e_ref[h * PEER_TOPK:(h + 1) * PEER_TOPK, :] = jnp.concatenate(top_e, axis=0)
        g_ref[h * PEER_TOPK:(h + 1) * PEER_TOPK, :] = g


def peer_route(x, wq_bf16, sk_bf16):
    n = x.shape[0]
    tt = min(PEER_ROUTE_TOKENS, n)
    assert n % tt == 0
    return pl.pallas_call(
        _peer_route_kernel,
        grid=(n // tt,),
        in_specs=[pl.BlockSpec((tt, D_MODEL), lambda i: (i, 0)),
                  pl.BlockSpec((D_MODEL, PEER_HEADS * PEER_DKEY), lambda i: (0, 0)),
                  pl.BlockSpec((2, PEER_NKEYS, PEER_HALF), lambda i: (0, 0, 0))],
        out_specs=[pl.BlockSpec((PEER_SLOTS, tt), lambda i: (0, i)),
                   pl.BlockSpec((PEER_SLOTS, tt), lambda i: (0, i))],
        out_shape=[jax.ShapeDtypeStruct((PEER_SLOTS, n), jnp.int32),
                   jax.ShapeDtypeStruct((PEER_SLOTS, n), jnp.float32)],
        compiler_params=pltpu.CompilerParams(
            dimension_semantics=("parallel",), vmem_limit_bytes=VMEM_LIMIT_BYTES),
        name="peer_route",
    )(x, wq_bf16, sk_bf16)


def _peer_gather_kernel(e_hbm, x_ref, g_ref, lng_ref, lnb_ref, uv_hbm, o_ref,
                        idx_smem, buf, sem_idx, sem_rows):
    i = pl.program_id(0)
    n = pl.num_programs(0)
    rows = PEER_GATHER_TOKENS * PEER_SLOTS
    groups = PEER_SLOTS // SUBLANES
    half = D_MODEL // LANES

    def idx_copy(tile, slot):
        return pltpu.make_async_copy(e_hbm.at[pl.ds(tile * rows, rows)], idx_smem.at[slot], sem_idx.at[slot])

    def row_copy(row, slot, r_hi, r_lo):
        return pltpu.make_async_copy(uv_hbm.at[row], buf.at[slot, r_hi, :, r_lo, :], sem_rows.at[slot])

    def rows_wait(slot):
        pltpu.make_async_copy(uv_hbm.at[pl.ds(0, rows)], buf.at[slot], sem_rows.at[slot]).wait()

    last = n - 1

    @pl.when(i == 0)
    def _():
        idx_copy(0, 0).start()
        idx_copy(0, 0).wait()

        def body(r, carry):
            row_copy(idx_smem[0, r], 0, r // SUBLANES, r % SUBLANES).start()
            return carry
        lax.fori_loop(0, rows, body, 0)
        idx_copy(jnp.minimum(1, last), 1).start()

    per_phase = PEER_SLOTS // (2 * half)

    def step(cur, nxt):
        idx_copy(0, nxt).wait()
        idx_copy(jnp.minimum(i + 2, last), cur).start()
        rows_wait(cur)
        g_tile = g_ref[0]
        for t in range(PEER_GATHER_TOKENS):
            def prefetch(phase):
                for r in range(t * PEER_SLOTS + phase * per_phase, t * PEER_SLOTS + (phase + 1) * per_phase):
                    row_copy(idx_smem[nxt, r], nxt, r // SUBLANES, r % SUBLANES).start()
            x_t = x_ref[t]
            lo = t * groups
            acc = None
            for lg in range(half):
                prefetch(lg)
                term = buf[cur, lo:lo + groups, lg] * x_t[lg:lg + 1, :]
                acc = term if acc is None else acc + term
            hid = jnp.sum(acc, axis=-1, keepdims=True)
            act = 0.5 * hid * (1.0 + lax.erf(hid * (2.0 ** -0.5)))
            wgt = g_tile[:, t:t + 1].reshape(groups, SUBLANES, 1) * act
            out_rows = []
            for lg in range(half):
                prefetch(half + lg)
                part = jnp.sum(buf[cur, lo:lo + groups, half + lg] * wgt, axis=0)
                out_rows.append(jnp.sum(part, axis=0, keepdims=True))
            z = DEEPNORM_ALPHA * x_t + jnp.concatenate(out_rows, axis=0)
            mu = jnp.sum(jnp.sum(z, axis=1, keepdims=True), axis=0, keepdims=True) * (1.0 / D_MODEL)
            zc = z - mu
            var = jnp.sum(jnp.sum(zc * zc, axis=1, keepdims=True), axis=0, keepdims=True) * (1.0 / D_MODEL)
            o_ref[t] = zc * lax.rsqrt(var + LN_EPS) * lng_ref[...] + lnb_ref[...]

        @pl.when(i == last)
        def _():
            rows_wait(nxt)
            idx_copy(0, cur).wait()

    @pl.when(i % 2 == 0)
    def _():
        step(0, 1)

    @pl.when(i % 2 == 1)
    def _():
        step(1, 0)


def peer_gather_ln(x, e, g, uv, ln_g, ln_b):
    n = x.shape[0]
    tb = PEER_GATHER_TOKENS
    assert n % tb == 0
    rows = tb * PEER_SLOTS
    lane_groups = D_MODEL // LANES
    g3 = g.reshape(n // tb, tb, PEER_SLOTS).transpose(0, 2, 1)
    y = pl.pallas_call(
        _peer_gather_kernel,
        grid=(n // tb,),
        in_specs=[pl.BlockSpec(memory_space=pl.ANY),
                  pl.BlockSpec((tb, lane_groups, LANES), lambda i: (i, 0, 0)),
                  pl.BlockSpec((1, PEER_SLOTS, tb), lambda i: (i, 0, 0)),
                  pl.BlockSpec((lane_groups, LANES), lambda i: (0, 0)),
                  pl.BlockSpec((lane_groups, LANES), lambda i: (0, 0)),
                  pl.BlockSpec(memory_space=pl.ANY)],
        out_specs=pl.BlockSpec((tb, lane_groups, LANES), lambda i: (i, 0, 0)),
        out_shape=jax.ShapeDtypeStruct((n, lane_groups, LANES), jnp.float32),
        scratch_shapes=[pltpu.SMEM((2, rows), jnp.int32),
                        pltpu.VMEM((2, rows // SUBLANES, 2 * lane_groups, SUBLANES, LANES), jnp.float32),
                        pltpu.SemaphoreType.DMA((2,)),
                        pltpu.SemaphoreType.DMA((2,))],
        compiler_params=pltpu.CompilerParams(
            dimension_semantics=("arbitrary",), vmem_limit_bytes=VMEM_LIMIT_BYTES),
        name="peer_gather",
    )(e.reshape(n * PEER_SLOTS), x.reshape(n, lane_groups, LANES), g3,
      ln_g.reshape(lane_groups, LANES), ln_b.reshape(lane_groups, LANES),
      uv.reshape(uv.shape[0], 2 * lane_groups, LANES))
    return y.reshape(n, D_MODEL)


def peer_ffn_ln(x, wq_bf16, sk_bf16, uv, ln_g, ln_b):
    B, T, D = x.shape
    xf = x.reshape(B * T, D)
    e_t, g_t = peer_route(xf, wq_bf16, sk_bf16)
    y = peer_gather_ln(xf, e_t.T, g_t.T, uv, ln_g, ln_b)
    return y.reshape(B, T, D)


def _moba_prompt_kernel(qt_ref, k_ref, vt_ref, o_ref, kmean_ref, sel_ref):
    qi = pl.program_id(2)
    nb = k_ref.shape[2]
    blk = k_ref.shape[3]
    scale = HEAD_DIM ** -0.5

    @pl.when(qi == 0)
    def _():
        for nblk in range(nb):
            kmean_ref[nblk:nblk + 1, :] = jnp.sum(k_ref[0, 0, nblk], axis=0, keepdims=True) * (1.0 / blk)

    qt = qt_ref[0, 0, 0].astype(jnp.bfloat16)
    gate = jnp.dot(kmean_ref[...].astype(jnp.bfloat16), qt, preferred_element_type=jnp.float32)
    blk_id = lax.broadcasted_iota(jnp.int32, gate.shape, 0)
    gate = jnp.where(blk_id < qi, gate, NEG_INF)
    rank = jnp.zeros(gate.shape, jnp.int32)
    for m in range(nb):
        gm = gate[m:m + 1, :]
        beats = (gm > gate) | ((gm == gate) & (blk_id > m))
        rank = rank + beats.astype(jnp.int32)
    sel_ref[...] = ((rank < MOBA_TOPK) & (blk_id < qi)).astype(jnp.float32)

    def attend(kb, vtb, keep, m_run, l_run, acc):
        s = jnp.dot(kb.astype(jnp.bfloat16), qt, preferred_element_type=jnp.float32) * scale
        s = jnp.where(keep, s, NEG_INF)
        m_new = jnp.maximum(m_run, jnp.max(s, axis=0, keepdims=True))
        alpha = jnp.exp(m_run - m_new)
        p = jnp.exp(s - m_new)
        l_new = alpha * l_run + jnp.sum(p, axis=0, keepdims=True)
        acc_new = alpha * acc + jnp.dot(vtb.astype(jnp.bfloat16), p.astype(jnp.bfloat16),
                                        preferred_element_type=jnp.float32)
        return m_new, l_new, acc_new

    key_pos = lax.broadcasted_iota(jnp.int32, (blk, blk), 0)
    qry_pos = lax.broadcasted_iota(jnp.int32, (blk, blk), 1)
    init = (jnp.full((1, blk), NEG_INF, jnp.float32), jnp.zeros((1, blk), jnp.float32),
            jnp.zeros((HEAD_DIM, blk), jnp.float32))
    carry = attend(k_ref[0, 0, qi], vt_ref[0, 0, qi], key_pos <= qry_pos, *init)

    def past_block(nblk, c):
        keep = sel_ref[pl.ds(nblk, 1), :] > 0.5
        return attend(k_ref[0, 0, nblk], vt_ref[0, 0, nblk], keep, *c)

    m_run, l_run, acc = lax.fori_loop(0, qi, past_block, carry)
    o_ref[0, 0, 0] = acc / l_run


def moba_prompt(q, k, v):
    B, T, _ = q.shape
    nb = T // MOBA_BLOCK
    split = lambda t: t.reshape(B, nb, MOBA_BLOCK, MOBA_HEADS, HEAD_DIM)
    qt = split(q).transpose(0, 3, 1, 4, 2)
    kb = split(k).transpose(0, 3, 1, 2, 4)
    vt = split(v).transpose(0, 3, 1, 4, 2)
    ot = pl.pallas_call(
        _moba_prompt_kernel,
        grid=(B, MOBA_HEADS, nb),
        in_specs=[pl.BlockSpec((1, 1, 1, HEAD_DIM, MOBA_BLOCK), lambda b, h, i: (b, h, i, 0, 0)),
                  pl.BlockSpec((1, 1, nb, MOBA_BLOCK, HEAD_DIM), lambda b, h, i: (b, h, 0, 0, 0)),
                  pl.BlockSpec((1, 1, nb, HEAD_DIM, MOBA_BLOCK), lambda b, h, i: (b, h, 0, 0, 0))],
        out_specs=pl.BlockSpec((1, 1, 1, HEAD_DIM, MOBA_BLOCK), lambda b, h, i: (b, h, i, 0, 0)),
        out_shape=jax.ShapeDtypeStruct((B, MOBA_HEADS, nb, HEAD_DIM, MOBA_BLOCK), jnp.float32),
        scratch_shapes=[pltpu.VMEM((nb, HEAD_DIM), jnp.float32),
                        pltpu.VMEM((nb, MOBA_BLOCK), jnp.float32)],
        compiler_params=pltpu.CompilerParams(
            dimension_semantics=("parallel", "parallel", "arbitrary"), vmem_limit_bytes=VMEM_LIMIT_BYTES),
        name="moba_prompt",
    )(qt, kb, vt)
    return ot.transpose(0, 2, 4, 1, 3).reshape(B, T, MOBA_HEADS * HEAD_DIM)


def layer_norm(x, g, b):
    xf = x.astype(jnp.float32)
    mu = jnp.mean(xf, -1, keepdims=True)
    var = jnp.mean(jnp.square(xf - mu), -1, keepdims=True)
    return ((xf - mu) * lax.rsqrt(var + LN_EPS)).astype(x.dtype) * g + b


def rope(x, pos):
    half = x.shape[-1] // 2
    inv = ROPE_THETA ** (-jnp.arange(half, dtype=jnp.float32) / half)
    ang = pos.astype(jnp.float32)[:, None] * inv[None, :]
    cos = jnp.cos(ang)[None, :, None, :]
    sin = jnp.sin(ang)[None, :, None, :]
    x1 = x[..., :half].astype(jnp.float32)
    x2 = x[..., half:].astype(jnp.float32)
    return jnp.concatenate([x1 * cos - x2 * sin, x2 * cos + x1 * sin], -1).astype(x.dtype)


WKV_TIME_CHUNK = 32


def _wkv_scan_kernel(w_ref, nkk_ref, kka_ref, k_ref, r_ref, v_ref, s0_ref, y_ref, sT_ref, s_ref):
    tc = pl.program_id(1)
    steps = w_ref.shape[0]
    nvi = s_ref.shape[0]

    @pl.when(tc == 0)
    def _():
        s_ref[...] = s0_ref[...]

    def step(t, carry):
        w = w_ref[t]
        nkk = nkk_ref[t]
        kka = kka_ref[t]
        kv = k_ref[t]
        r = r_ref[t]
        for vi in range(nvi):
            s = s_ref[vi]
            sa = jnp.sum(s * nkk, axis=0, keepdims=True)
            s_new = s * w + sa * kka + v_ref[t, vi:vi + 1, :] * kv
            s_ref[vi] = s_new
            y_ref[t, vi:vi + 1, :] = jnp.sum(s_new * r, axis=0, keepdims=True)
        return carry

    lax.fori_loop(0, steps, step, 0)

    @pl.when(tc == pl.num_programs(1) - 1)
    def _():
        sT_ref[...] = s_ref[...]


def wkv_scan_pallas(r, decay, k, v, kk, a, s0):
    B, T, H, N = r.shape
    bh = B * H
    vh = max(1, LANES // bh)
    nvi = N // vh
    width = vh * bh
    assert width % LANES == 0
    tcs = min(WKV_TIME_CHUNK, T)
    assert T % tcs == 0

    def key_major(x):
        xt = jnp.transpose(x, (1, 3, 0, 2)).reshape(T, N, bh)
        return jnp.concatenate([xt] * vh, axis=-1)

    v_l = jnp.transpose(v.reshape(B, T, H, vh, nvi), (1, 4, 3, 0, 2)).reshape(T, nvi, width)
    s0_l = jnp.transpose(s0.reshape(B, H, vh, nvi, N), (3, 4, 2, 0, 1)).reshape(nvi, N, width)
    kspec = pl.BlockSpec((tcs, N, LANES), lambda j, t: (t, 0, j))
    vspec = pl.BlockSpec((tcs, nvi, LANES), lambda j, t: (t, 0, j))
    sspec = pl.BlockSpec((nvi, N, LANES), lambda j, t: (0, 0, j))
    y_l, sT_l = pl.pallas_call(
        _wkv_scan_kernel,
        grid=(width // LANES, T // tcs),
        in_specs=[kspec, kspec, kspec, kspec, kspec, vspec, sspec],
        out_specs=[vspec, sspec],
        out_shape=[jax.ShapeDtypeStruct((T, nvi, width), jnp.float32),
                   jax.ShapeDtypeStruct((nvi, N, width), jnp.float32)],
        scratch_shapes=[pltpu.VMEM((nvi, N, LANES), jnp.float32)],
        compiler_params=pltpu.CompilerParams(
            dimension_semantics=("parallel", "arbitrary"), vmem_limit_bytes=VMEM_LIMIT_BYTES),
        name="wkv_scan",
    )(key_major(decay), key_major(-kk), key_major(kk * a), key_major(k), key_major(r), v_l, s0_l)
    y = jnp.transpose(y_l.reshape(T, nvi, vh, B, H), (3, 0, 4, 2, 1)).reshape(B, T, H, N)
    sT = jnp.transpose(sT_l.reshape(nvi, N, vh, B, H), (3, 4, 2, 0, 1)).reshape(B, H, N, N)
    return y, sT


def rwkv_group(proj, prev, s0, mu, w0, w2, a0, a2, g2, k_k, k_a, r_k, lnx_g, lnx_b):
    B, T, _ = proj.shape
    shifted = jnp.concatenate([prev[:, None, :].astype(proj.dtype), proj[:, :-1]], axis=1)
    xm = proj + mu * (shifted - proj)
    c1, c2, c3 = RWKV_WIDTH, 2 * RWKV_WIDTH, 3 * RWKV_WIDTH
    c4 = c3 + LORA_W
    c5 = c4 + LORA_A
    r, k, v = xm[..., :c1], xm[..., c1:c2], xm[..., c2:c3]
    xw, xa, xg = xm[..., c3:c4], xm[..., c4:c5], xm[..., c5:]
    w = -jax.nn.softplus(-(w0 + jnp.tanh(xw) @ w2)) - 0.5
    decay = jnp.exp(-jnp.exp(w.astype(jnp.float32)))
    a = jax.nn.sigmoid(a0 + xa @ a2)
    g = jax.nn.sigmoid(xg) @ g2
    heads = lambda t: t.astype(jnp.float32).reshape(B, T, RWKV_HEADS, HEAD_DIM)
    kk = heads(k * k_k)
    kk = kk / jnp.maximum(jnp.sqrt(jnp.sum(kk * kk, -1, keepdims=True)), 1e-12)
    k = k * (1.0 + (a - 1.0) * k_a)
    r_h, k_h, v_h, a_h, w_h = heads(r), heads(k), heads(v), heads(a), heads(decay)
    y, s = wkv_scan_pallas(r_h, w_h, k_h, v_h, kk, a_h, s0.astype(jnp.float32))
    mean = jnp.mean(y, -1, keepdims=True)
    var = jnp.mean(jnp.square(y - mean), -1, keepdims=True)
    y = ((y - mean) * lax.rsqrt(var + GN_EPS)).reshape(B, T, RWKV_WIDTH) * lnx_g + lnx_b
    bonus = (jnp.sum(r_h * k_h * r_k, -1, keepdims=True) * v_h).reshape(B, T, RWKV_WIDTH)
    out = ((y + bonus) * g).astype(proj.dtype)
    return out, s.astype(s0.dtype), proj[:, -1]


def moba_attend(q, k_all, v_all, q_pos0):
    B, Tq, H, hd = q.shape
    L = k_all.shape[1]
    qc = QUERY_BLOCK if Tq >= QUERY_BLOCK else Tq
    n_chunks = -(-Tq // qc)
    tq_pad = n_chunks * qc
    n_blocks = -(-(q_pos0 + tq_pad) // MOBA_BLOCK)
    l_pad = n_blocks * MOBA_BLOCK
    k_sel = min(MOBA_TOPK, n_blocks)
    q = jnp.pad(q, ((0, 0), (0, tq_pad - Tq), (0, 0), (0, 0)))
    kb = jnp.pad(k_all, ((0, 0), (0, l_pad - L), (0, 0), (0, 0))).reshape(B, n_blocks, MOBA_BLOCK, H, hd)
    vb = jnp.pad(v_all, ((0, 0), (0, l_pad - L), (0, 0), (0, 0))).reshape(B, n_blocks, MOBA_BLOCK, H, hd)
    k_mean = jnp.mean(kb, axis=2, dtype=jnp.float32)
    qs = jnp.moveaxis(q.reshape(B, n_chunks, qc, H, hd), 1, 0)
    scale = hd ** -0.5
    b_idx = jnp.arange(B)[:, None, None, None]
    h_idx = jnp.arange(H)[None, :, None, None]
    blk_ids = jnp.arange(n_blocks)
    offs = jnp.arange(MOBA_BLOCK)

    def chunk_fn(args):
        c, qx = args
        start = q_pos0 + c * qc
        cur = start // MOBA_BLOCK
        pos = start + jnp.arange(qc)
        gate = jnp.einsum('bqhd,bnhd->bhqn', qx.astype(jnp.float32), k_mean)
        gate = jnp.where(blk_ids < cur, gate, NEG_INF)
        _, sel = lax.top_k(gate, k_sel)
        sel_ok = sel < cur
        ks = kb[b_idx, sel, :, h_idx]
        vs = vb[b_idx, sel, :, h_idx]
        s_sel = jnp.einsum('bqhd,bhqnjd->bhqnj', qx, ks, preferred_element_type=jnp.float32) * scale
        s_sel = jnp.where(sel_ok[..., None], s_sel, NEG_INF).reshape(B, H, qc, k_sel * MOBA_BLOCK)
        ko = lax.dynamic_index_in_dim(kb, cur, axis=1, keepdims=False)
        vo = lax.dynamic_index_in_dim(vb, cur, axis=1, keepdims=False)
        s_own = jnp.einsum('bqhd,bjhd->bhqj', qx, ko, preferred_element_type=jnp.float32) * scale
        s_own = jnp.where((cur * MOBA_BLOCK + offs)[None, :] <= pos[:, None], s_own, NEG_INF)
        p = jax.nn.softmax(jnp.concatenate([s_sel, s_own], -1), axis=-1).astype(qx.dtype)
        p_sel = p[..., :k_sel * MOBA_BLOCK].reshape(B, H, qc, k_sel, MOBA_BLOCK)
        p_own = p[..., k_sel * MOBA_BLOCK:]
        return (jnp.einsum('bhqnj,bhqnjd->bqhd', p_sel, vs)
                + jnp.einsum('bhqj,bjhd->bqhd', p_own, vo))

    out = lax.map(chunk_fn, (jnp.arange(n_chunks), qs))
    return jnp.moveaxis(out, 0, 1).reshape(B, tq_pad, H, hd)[:, :Tq]


def mem_attend(x, mk, mv, wq, wo):
    B, T, _ = x.shape
    q = pmatmul(x.reshape(B * T, D_MODEL), wq).reshape(B, T, MEM_HEADS, MEM_HEAD_DIM)
    s = jnp.einsum('bthd,bmhd->bhtm', q, mk.astype(q.dtype), preferred_element_type=jnp.float32) * MEM_HEAD_DIM ** -0.5
    prob = jax.nn.softmax(s, axis=-1).astype(x.dtype)
    o = jnp.einsum('bhtm,bmhd->bthd', prob, mv.astype(x.dtype)).reshape(B * T, D_MODEL)
    return pmatmul(o, wo).reshape(B, T, D_MODEL)


def peer_ffn(x, wq, sub_keys, u_tab, v_tab):
    B, T, D = x.shape
    n = B * T
    chunk = min(PEER_CHUNK, n)
    n_ch = -(-n // chunk)
    xp = jnp.pad(x.reshape(n, D), ((0, n_ch * chunk - n), (0, 0))).reshape(n_ch, chunk, D)

    def chunk_fn(xc):
        q = (xc @ wq).reshape(chunk, PEER_HEADS, 2, PEER_HALF)
        s1 = jnp.einsum('chd,kd->chk', q[:, :, 0], sub_keys[0], preferred_element_type=jnp.float32)
        s2 = jnp.einsum('chd,kd->chk', q[:, :, 1], sub_keys[1], preferred_element_type=jnp.float32)
        v1, i1 = lax.top_k(s1, PEER_TOPK)
        v2, i2 = lax.top_k(s2, PEER_TOPK)
        cand = (v1[..., :, None] + v2[..., None, :]).reshape(chunk, PEER_HEADS, PEER_TOPK * PEER_TOPK)
        cand_idx = (i1[..., :, None] * PEER_NKEYS + i2[..., None, :]).reshape(chunk, PEER_HEADS, PEER_TOPK * PEER_TOPK)
        top_s, top_pos = lax.top_k(cand, PEER_TOPK)
        e = jnp.take_along_axis(cand_idx, top_pos, axis=-1)
        g = jax.nn.softmax(top_s, axis=-1)
        hid = jax.nn.gelu(jnp.einsum('cd,ched->che', xc, u_tab[e]), approximate=False)
        wgt = (g * hid).astype(xc.dtype)
        return jnp.einsum('che,ched->cd', wgt, v_tab[e])

    out = lax.map(chunk_fn, xp).reshape(n_ch * chunk, D)[:n]
    return out.reshape(B, T, D)


def gather_pages(pool, page_table):
    g = pool[page_table]
    return g.reshape(g.shape[0], g.shape[1] * g.shape[2], g.shape[3], g.shape[4])


def run_trunk(x, pos0, shift0, wkv0, cache_k, cache_v, page_table, mem_k, mem_v, p):
    B, T, _ = x.shape
    pos = pos0 + jnp.arange(T, dtype=jnp.int32)
    k_rows, v_rows, wkvs, shifts = [], [], [], []
    for i in range(DEPTH):
        proj = pmatmul(x.reshape(B * T, D_MODEL), p['w_in'][i]).reshape(B, T, IN_PROJ)
        rw_out, wkv_i, shift_i = rwkv_group(
            proj[..., :RWKV_PROJ], shift0[i], wkv0[i], p['shift_mu'][i],
            p['rwkv_w0'][i], p['rwkv_w2'][i], p['rwkv_a0'][i], p['rwkv_a2'][i],
            p['rwkv_g2'][i], p['rwkv_k_k'][i], p['rwkv_k_a'][i], p['rwkv_r_k'][i],
            p['rwkv_lnx_g'][i], p['rwkv_lnx_b'][i])
        q, k, v = jnp.split(proj[..., RWKV_PROJ:], 3, axis=-1)
        q = rope(q.reshape(B, T, MOBA_HEADS, HEAD_DIM), pos)
        k = rope(k.reshape(B, T, MOBA_HEADS, HEAD_DIM), pos)
        v = v.reshape(B, T, MOBA_HEADS, HEAD_DIM)
        if cache_k is None:
            mb_out = moba_prompt(q.reshape(B, T, MOBA_WIDTH), k.reshape(B, T, MOBA_WIDTH),
                                 v.reshape(B, T, MOBA_WIDTH))
        else:
            k_all = jnp.concatenate([gather_pages(cache_k[i], page_table).astype(k.dtype), k], axis=1)
            v_all = jnp.concatenate([gather_pages(cache_v[i], page_table).astype(v.dtype), v], axis=1)
            mb_out = moba_attend(q, k_all, v_all, pos0).reshape(B, T, MOBA_WIDTH)
        mix = pmatmul(jnp.concatenate([rw_out, mb_out], axis=-1).reshape(B * T, D_MODEL),
                      p['w_out'][i]).reshape(B, T, D_MODEL)
        x = layer_norm(DEEPNORM_ALPHA * x + mix, p['ln1_g'][i], p['ln1_b'][i])
        cross = mem_attend(x, mem_k[i], mem_v[i], p['w_mem_q'][i], p['w_mem_o'][i])
        x = layer_norm(DEEPNORM_ALPHA * x + cross, p['ln2_g'][i], p['ln2_b'][i])
        x = peer_ffn_ln(x, p['peer_w_q_bf16'][i], p['peer_sub_keys_bf16'][i], p['peer_uv'][i],
                        p['ln3_g'][i], p['ln3_b'][i])
        k_rows.append(k)
        v_rows.append(v)
        wkvs.append(wkv_i)
        shifts.append(shift_i)
    return x, jnp.stack(k_rows), jnp.stack(v_rows), jnp.stack(wkvs), jnp.stack(shifts)


def kernel(x_prompt, x_sample, mem_prompt, cache_k, cache_v, page_table, state_wkv,
           state_shift, cache_mem_k, cache_mem_v, w_in, shift_mu, rwkv_w0, rwkv_w2,
           rwkv_a0, rwkv_a2, rwkv_g2, rwkv_k_k, rwkv_k_a, rwkv_r_k, rwkv_lnx_g,
           rwkv_lnx_b, w_out, ln1_g, ln1_b, w_mem_q, w_mem_kv, w_mem_o, ln2_g, ln2_b,
           peer_w_q, peer_sub_keys, peer_u, peer_v, ln3_g, ln3_b):
    p = dict(w_in=w_in, shift_mu=shift_mu, rwkv_w0=rwkv_w0, rwkv_w2=rwkv_w2,
             rwkv_a0=rwkv_a0, rwkv_a2=rwkv_a2, rwkv_g2=rwkv_g2, rwkv_k_k=rwkv_k_k,
             rwkv_k_a=rwkv_k_a, rwkv_r_k=rwkv_r_k, rwkv_lnx_g=rwkv_lnx_g,
             rwkv_lnx_b=rwkv_lnx_b, w_out=w_out, ln1_g=ln1_g, ln1_b=ln1_b,
             w_mem_q=w_mem_q, w_mem_o=w_mem_o, ln2_g=ln2_g, ln2_b=ln2_b,
             peer_w_q=peer_w_q, peer_sub_keys=peer_sub_keys, peer_u=peer_u,
             peer_v=peer_v, ln3_g=ln3_g, ln3_b=ln3_b)
    p['peer_uv'] = jnp.concatenate([peer_u, peer_v], axis=-1)
    p['peer_w_q_bf16'] = peer_w_q.astype(jnp.bfloat16)
    p['peer_sub_keys_bf16'] = peer_sub_keys.astype(jnp.bfloat16)
    B = x_prompt.shape[0]
    mem_len = mem_prompt.shape[1]
    mkv = jnp.stack([pmatmul(mem_prompt.reshape(B * mem_len, D_MODEL), w_mem_kv[i]).reshape(B, mem_len, 2 * D_MODEL)
                     for i in range(DEPTH)])
    mem_k_prompt = mkv[..., :D_MODEL].reshape(DEPTH, B, mem_len, MEM_HEADS, MEM_HEAD_DIM)
    mem_v_prompt = mkv[..., D_MODEL:].reshape(DEPTH, B, mem_len, MEM_HEADS, MEM_HEAD_DIM)
    shift0 = jnp.zeros((DEPTH, B, RWKV_PROJ), x_prompt.dtype)
    wkv0 = jnp.zeros((DEPTH, B, RWKV_HEADS, HEAD_DIM, HEAD_DIM), x_prompt.dtype)
    y_prompt, k_prompt, v_prompt, wkv_prompt, shift_prompt = run_trunk(
        x_prompt, 0, shift0, wkv0, None, None, None, mem_k_prompt, mem_v_prompt, p)
    past_len = page_table.shape[1] * PAGE_SIZE
    y_sample, k_sample, v_sample, wkv_sample, shift_sample = run_trunk(
        x_sample, past_len, state_shift, state_wkv, cache_k, cache_v, page_table,
        cache_mem_k, cache_mem_v, p)
    return (y_prompt, y_sample, k_prompt, v_prompt, wkv_prompt, shift_prompt,
            mem_k_prompt, mem_v_prompt, k_sample, v_sample, wkv_sample, shift_sample)
```

```python
import functools

import jax
import jax.numpy as jnp
from jax import lax
from jax.experimental import pallas as pl
from jax.experimental.pallas import tpu as pltpu

D_MODEL = 1024
DEPTH = 2
PAGE_SIZE = 128
HEAD_DIM = 64
RWKV_WIDTH = D_MODEL // 2
MOBA_WIDTH = D_MODEL - RWKV_WIDTH
RWKV_HEADS = RWKV_WIDTH // HEAD_DIM
MOBA_HEADS = MOBA_WIDTH // HEAD_DIM
LORA_W = max(32, int(round(1.8 * D_MODEL ** 0.5 / 32)) * 32)
LORA_A = max(32, int(round(1.8 * D_MODEL ** 0.5 / 32)) * 32)
LORA_G = max(32, int(round(0.6 * D_MODEL ** 0.8 / 32)) * 32)
RWKV_PROJ = 3 * RWKV_WIDTH + LORA_W + LORA_A + LORA_G
MOBA_PROJ = 3 * MOBA_WIDTH
IN_PROJ = RWKV_PROJ + MOBA_PROJ
GN_EPS = 64e-5
MOBA_BLOCK = 256
MOBA_TOPK = 3
ROPE_THETA = 10000.0
NEG_INF = -1e30
MEM_HEADS = 4
MEM_HEAD_DIM = D_MODEL // MEM_HEADS
PEER_HEADS = 8
PEER_NKEYS = 128
PEER_TOPK = 16
PEER_DKEY = 256
PEER_HALF = PEER_DKEY // 2
DEEPNORM_ALPHA = (2 * DEPTH) ** 0.25
LN_EPS = 1e-5

VMEM_LIMIT_BYTES = 56 * 1024 * 1024
SUBLANES = 8
LANES = 128


def _matmul_kernel(x_ref, w_ref, o_ref):
    o_ref[...] = jnp.dot(x_ref[...].astype(jnp.bfloat16), w_ref[...].astype(jnp.bfloat16),
                         preferred_element_type=jnp.float32)


def pmatmul(x, w, tm=512, tn=512):
    m, k = x.shape
    n = w.shape[1]
    tm = min(tm, m)
    assert m % tm == 0
    if n % tn != 0:
        tn = n
    return pl.pallas_call(
        _matmul_kernel,
        grid=(m // tm, n // tn),
        in_specs=[pl.BlockSpec((tm, k), lambda i, j: (i, 0)),
                  pl.BlockSpec((k, tn), lambda i, j: (0, j))],
        out_specs=pl.BlockSpec((tm, tn), lambda i, j: (i, j)),
        out_shape=jax.ShapeDtypeStruct((m, n), jnp.float32),
        compiler_params=pltpu.CompilerParams(
            dimension_semantics=("parallel", "parallel"),
            vmem_limit_bytes=VMEM_LIMIT_BYTES),
    )(x, w)


PEER_ROUTE_TOKENS = 256
PEER_GATHER_TOKENS = 8
PEER_SLOTS = PEER_HEADS * PEER_TOPK


def _topk_sublane(s, k):
    n_rows = s.shape[0]
    iota = lax.broadcasted_iota(jnp.int32, s.shape, 0)
    vals, idxs = [], []
    for _ in range(k):
        m = jnp.max(s, axis=0, keepdims=True)
        idx = jnp.min(jnp.where(s == m, iota, n_rows), axis=0, keepdims=True)
        vals.append(m)
        idxs.append(idx)
        s = jnp.where(iota == idx, -jnp.inf, s)
    return vals, idxs


def _peer_route_kernel(x_ref, wq_ref, sk_ref, e_ref, g_ref):
    q = jnp.dot(x_ref[...].astype(jnp.bfloat16), wq_ref[...], preferred_element_type=jnp.float32)
    nt = (((1,), (1,)), ((), ()))
    for h in range(PEER_HEADS):
        base = h * PEER_DKEY
        q1 = q[:, base:base + PEER_HALF].astype(jnp.bfloat16)
        q2 = q[:, base + PEER_HALF:base + PEER_DKEY].astype(jnp.bfloat16)
        s1 = lax.dot_general(sk_ref[0], q1, nt, preferred_element_type=jnp.float32)
        s2 = lax.dot_general(sk_ref[1], q2, nt, preferred_element_type=jnp.float32)
        v1_rows, i1_rows = _topk_sublane(s1, PEER_TOPK)
        v2_rows, i2_rows = _topk_sublane(s2, PEER_TOPK)
        v2 = jnp.concatenate(v2_rows, axis=0)
        i2 = jnp.concatenate(i2_rows, axis=0)
        tt = s1.shape[1]
        v2s = [v2] + [jnp.concatenate(v2_rows[:SUBLANES], axis=0)] * (PEER_TOPK - 1)
        i2s = [i2] + [jnp.concatenate(i2_rows[:SUBLANES], axis=0)] * (PEER_TOPK - 1)
        subs = ([lax.broadcasted_iota(jnp.int32, (PEER_TOPK, tt), 0)]
                + [lax.broadcasted_iota(jnp.int32, (SUBLANES, tt), 0)] * (PEER_TOPK - 1))
        cand = jnp.concatenate([v1_rows[a] + v2s[a] for a in range(PEER_TOPK)], axis=0)
        cand_idx = jnp.concatenate([i1_rows[a] * PEER_NKEYS + i2s[a] for a in range(PEER_TOPK)], axis=0)
        pos_all = jnp.concatenate([a * PEER_TOPK + subs[a] for a in range(PEER_TOPK)], axis=0)
        n_cand = PEER_TOPK * PEER_TOPK
        top_s, top_e = [], []
        for _ in range(PEER_TOPK):
            m = jnp.max(cand, axis=0, keepdims=True)
            pos = jnp.min(jnp.where(cand == m, pos_all, n_cand), axis=0, keepdims=True)
            hit = pos_all == pos
            top_s.append(m)
            top_e.append(jnp.sum(jnp.where(hit, cand_idx, 0), axis=0, keepdims=True))
            cand = jnp.where(hit, -jnp.inf, cand)
        top_s = jnp.concatenate(top_s, axis=0)
        p = jnp.exp(top_s - top_s[0:1, :])
        g = p / jnp.sum(p, axis=0, keepdims=True)
        e_ref[h * PEER_TOPK:(h + 1) * PEER_TOPK, :] = jnp.concatenate(top_e, axis=0)
        g_ref[h * PEER_TOPK:(h + 1) * PEER_TOPK, :] = g


def peer_route(x, wq_bf16, sk_bf16):
    n = x.shape[0]
    tt = min(PEER_ROUTE_TOKENS, n)
    assert n % tt == 0
    return pl.pallas_call(
        _peer_route_kernel,
        grid=(n // tt,),
        in_specs=[pl.BlockSpec((tt, D_MODEL), lambda i: (i, 0)),
                  pl.BlockSpec((D_MODEL, PEER_HEADS * PEER_DKEY), lambda i: (0, 0)),
                  pl.BlockSpec((2, PEER_NKEYS, PEER_HALF), lambda i: (0, 0, 0))],
        out_specs=[pl.BlockSpec((PEER_SLOTS, tt), lambda i: (0, i)),
                   pl.BlockSpec((PEER_SLOTS, tt), lambda i: (0, i))],
        out_shape=[jax.ShapeDtypeStruct((PEER_SLOTS, n), jnp.int32),
                   jax.ShapeDtypeStruct((PEER_SLOTS, n), jnp.float32)],
        compiler_params=pltpu.CompilerParams(
            dimension_semantics=("parallel",), vmem_limit_bytes=VMEM_LIMIT_BYTES),
        name="peer_route",
    )(x, wq_bf16, sk_bf16)


def _peer_gather_kernel(e_hbm, x_ref, g_ref, lng_ref, lnb_ref, uv_hbm, o_ref,
                        idx_smem, buf, sem_idx, sem_rows):
    i = pl.program_id(0)
    n = pl.num_programs(0)
    rows = PEER_GATHER_TOKENS * PEER_SLOTS
    groups = PEER_SLOTS // SUBLANES
    half = D_MODEL // LANES

    def idx_copy(tile, slot):
        return pltpu.make_async_copy(e_hbm.at[pl.ds(tile * rows, rows)], idx_smem.at[slot], sem_idx.at[slot])

    def row_copy(row, slot, r_hi, r_lo):
        return pltpu.make_async_copy(uv_hbm.at[row], buf.at[slot, r_hi, :, r_lo, :], sem_rows.at[slot])

    def rows_wait(slot):
        pltpu.make_async_copy(uv_hbm.at[pl.ds(0, rows)], buf.at[slot], sem_rows.at[slot]).wait()

    last = n - 1

    @pl.when(i == 0)
    def _():
        idx_copy(0, 0).start()
        idx_copy(0, 0).wait()

        def body(r, carry):
            row_copy(idx_smem[0, r], 0, r // SUBLANES, r % SUBLANES).start()
            return carry
        lax.fori_loop(0, rows, body, 0)
        idx_copy(jnp.minimum(1, last), 1).start()

    per_phase = PEER_SLOTS // (2 * half)

    def step(cur, nxt):
        idx_copy(0, nxt).wait()
        idx_copy(jnp.minimum(i + 2, last), cur).start()
        rows_wait(cur)
        g_tile = g_ref[0]
        for t in range(PEER_GATHER_TOKENS):
            def prefetch(phase):
                for r in range(t * PEER_SLOTS + phase * per_phase, t * PEER_SLOTS + (phase + 1) * per_phase):
                    row_copy(idx_smem[nxt, r], nxt, r // SUBLANES, r % SUBLANES).start(priority=r % 2)
            x_t = x_ref[t]
            lo = t * groups
            acc = None
            for lg in range(half):
                prefetch(lg)
                term = buf[cur, lo:lo + groups, lg] * x_t[lg:lg + 1, :]
                acc = term if acc is None else acc + term
            hid = jnp.sum(acc, axis=-1, keepdims=True)
            act = 0.5 * hid * (1.0 + lax.erf(hid * (2.0 ** -0.5)))
            wgt = g_tile[:, t:t + 1].reshape(groups, SUBLANES, 1) * act
            out_rows = []
            for lg in range(half):
                prefetch(half + lg)
                part = jnp.sum(buf[cur, lo:lo + groups, half + lg] * wgt, axis=0)
                out_rows.append(jnp.sum(part, axis=0, keepdims=True))
            z = DEEPNORM_ALPHA * x_t + jnp.concatenate(out_rows, axis=0)
            mu = jnp.sum(jnp.sum(z, axis=1, keepdims=True), axis=0, keepdims=True) * (1.0 / D_MODEL)
            zc = z - mu
            var = jnp.sum(jnp.sum(zc * zc, axis=1, keepdims=True), axis=0, keepdims=True) * (1.0 / D_MODEL)
            o_ref[t] = zc * lax.rsqrt(var + LN_EPS) * lng_ref[...] + lnb_ref[...]

        @pl.when(i == last)
        def _():
            rows_wait(nxt)
            idx_copy(0, cur).wait()

    @pl.when(i % 2 == 0)
    def _():
        step(0, 1)

    @pl.when(i % 2 == 1)
    def _():
        step(1, 0)


def peer_gather_ln(x, e, g, uv, ln_g, ln_b):
    n = x.shape[0]
    tb = PEER_GATHER_TOKENS
    assert n % tb == 0 and n // tb >= 2
    rows = tb * PEER_SLOTS
    lane_groups = D_MODEL // LANES
    g3 = g.reshape(n // tb, tb, PEER_SLOTS).transpose(0, 2, 1)
    y = pl.pallas_call(
        _peer_gather_kernel,
        grid=(n // tb,),
        in_specs=[pl.BlockSpec(memory_space=pl.ANY),
                  pl.BlockSpec((tb, lane_groups, LANES), lambda i: (i, 0, 0)),
                  pl.BlockSpec((1, PEER_SLOTS, tb), lambda i: (i, 0, 0)),
                  pl.BlockSpec((lane_groups, LANES), lambda i: (0, 0)),
                  pl.BlockSpec((lane_groups, LANES), lambda i: (0, 0)),
                  pl.BlockSpec(memory_space=pl.ANY)],
        out_specs=pl.BlockSpec((tb, lane_groups, LANES), lambda i: (i, 0, 0)),
        out_shape=jax.ShapeDtypeStruct((n, lane_groups, LANES), jnp.float32),
        scratch_shapes=[pltpu.SMEM((2, rows), jnp.int32),
                        pltpu.VMEM((2, rows // SUBLANES, 2 * lane_groups, SUBLANES, LANES), jnp.float32),
                        pltpu.SemaphoreType.DMA((2,)),
                        pltpu.SemaphoreType.DMA((2,))],
        compiler_params=pltpu.CompilerParams(
            dimension_semantics=("arbitrary",), vmem_limit_bytes=VMEM_LIMIT_BYTES),
        name="peer_gather",
    )(e.reshape(n * PEER_SLOTS), x.reshape(n, lane_groups, LANES), g3,
      ln_g.reshape(lane_groups, LANES), ln_b.reshape(lane_groups, LANES),
      uv.reshape(uv.shape[0], 2 * lane_groups, LANES))
    return y.reshape(n, D_MODEL)


def peer_ffn_ln(x, wq_bf16, sk_bf16, uv, ln_g, ln_b):
    B, T, D = x.shape
    xf = x.reshape(B * T, D)
    e_t, g_t = peer_route(xf, wq_bf16, sk_bf16)
    y = peer_gather_ln(xf, e_t.T, g_t.T, uv, ln_g, ln_b)
    return y.reshape(B, T, D)


def _moba_prompt_kernel(qt_ref, k_ref, vt_ref, o_ref, kmean_ref, sel_ref):
    qi = pl.program_id(2)
    nb = k_ref.shape[2]
    blk = k_ref.shape[3]
    scale = HEAD_DIM ** -0.5

    @pl.when(qi == 0)
    def _():
        for nblk in range(nb):
            kmean_ref[nblk:nblk + 1, :] = jnp.sum(k_ref[0, 0, nblk], axis=0, keepdims=True) * (1.0 / blk)

    qt = qt_ref[0, 0, 0].astype(jnp.bfloat16)
    gate = jnp.dot(kmean_ref[...].astype(jnp.bfloat16), qt, preferred_element_type=jnp.float32)
    blk_id = lax.broadcasted_iota(jnp.int32, gate.shape, 0)
    gate = jnp.where(blk_id < qi, gate, NEG_INF)
    rank = jnp.zeros(gate.shape, jnp.int32)
    for m in range(nb):
        gm = gate[m:m + 1, :]
        beats = (gm > gate) | ((gm == gate) & (blk_id > m))
        rank = rank + beats.astype(jnp.int32)
    sel_ref[...] = ((rank < MOBA_TOPK) & (blk_id < qi)).astype(jnp.float32)

    def attend(kb, vtb, keep, m_run, l_run, acc):
        s = jnp.dot(kb.astype(jnp.bfloat16), qt, preferred_element_type=jnp.float32) * scale
        s = jnp.where(keep, s, NEG_INF)
        m_new = jnp.maximum(m_run, jnp.max(s, axis=0, keepdims=True))
        alpha = jnp.exp(m_run - m_new)
        p = jnp.exp(s - m_new)
        l_new = alpha * l_run + jnp.sum(p, axis=0, keepdims=True)
        acc_new = alpha * acc + jnp.dot(vtb.astype(jnp.bfloat16), p.astype(jnp.bfloat16),
                                        preferred_element_type=jnp.float32)
        return m_new, l_new, acc_new

    key_pos = lax.broadcasted_iota(jnp.int32, (blk, blk), 0)
    qry_pos = lax.broadcasted_iota(jnp.int32, (blk, blk), 1)
    init = (jnp.full((1, blk), NEG_INF, jnp.float32), jnp.zeros((1, blk), jnp.float32),
            jnp.zeros((HEAD_DIM, blk), jnp.float32))
    carry = attend(k_ref[0, 0, qi], vt_ref[0, 0, qi], key_pos <= qry_pos, *init)

    def past_block(nblk, c):
        keep = sel_ref[pl.ds(nblk, 1), :] > 0.5
        return attend(k_ref[0, 0, nblk], vt_ref[0, 0, nblk], keep, *c)

    m_run, l_run, acc = lax.fori_loop(0, qi, past_block, carry)
    o_ref[0, 0, 0] = acc / l_run


def moba_prompt(q, k, v):
    B, T, _ = q.shape
    nb = T // MOBA_BLOCK
    split = lambda t: t.reshape(B, nb, MOBA_BLOCK, MOBA_HEADS, HEAD_DIM)
    qt = split(q).transpose(0, 3, 1, 4, 2)
    kb = split(k).transpose(0, 3, 1, 2, 4)
    vt = split(v).transpose(0, 3, 1, 4, 2)
    ot = pl.pallas_call(
        _moba_prompt_kernel,
        grid=(B, MOBA_HEADS, nb),
        in_specs=[pl.BlockSpec((1, 1, 1, HEAD_DIM, MOBA_BLOCK), lambda b, h, i: (b, h, i, 0, 0)),
                  pl.BlockSpec((1, 1, nb, MOBA_BLOCK, HEAD_DIM), lambda b, h, i: (b, h, 0, 0, 0)),
                  pl.BlockSpec((1, 1, nb, HEAD_DIM, MOBA_BLOCK), lambda b, h, i: (b, h, 0, 0, 0))],
        out_specs=pl.BlockSpec((1, 1, 1, HEAD_DIM, MOBA_BLOCK), lambda b, h, i: (b, h, i, 0, 0)),
        out_shape=jax.ShapeDtypeStruct((B, MOBA_HEADS, nb, HEAD_DIM, MOBA_BLOCK), jnp.float32),
        scratch_shapes=[pltpu.VMEM((nb, HEAD_DIM), jnp.float32),
                        pltpu.VMEM((nb, MOBA_BLOCK), jnp.float32)],
        compiler_params=pltpu.CompilerParams(
            dimension_semantics=("parallel", "parallel", "arbitrary"), vmem_limit_bytes=VMEM_LIMIT_BYTES),
        name="moba_prompt",
    )(qt, kb, vt)
    return ot.transpose(0, 2, 4, 1, 3).reshape(B, T, MOBA_HEADS * HEAD_DIM)


def _moba_gate_kernel(pt_ref, q_ref, kpage_ref, sel_ref, acc_ref, gate_ref):
    del pt_ref
    p = pl.program_id(1)
    n_pages = pl.num_programs(1)
    pages_per_block = MOBA_BLOCK // PAGE_SIZE
    lane = lax.broadcasted_iota(jnp.int32, (MOBA_HEADS, LANES), 1)
    page_sum = jnp.sum(kpage_ref[0, 0], axis=-1, keepdims=True)

    @pl.when(p == 0)
    def _():
        gate_ref[...] = jnp.full(gate_ref.shape, NEG_INF, jnp.float32)

    @pl.when(p % pages_per_block == 0)
    def _():
        acc_ref[...] = page_sum

    @pl.when(p % pages_per_block != 0)
    def _():
        acc_ref[...] += page_sum

    @pl.when(p % pages_per_block == pages_per_block - 1)
    def _():
        k_mean = (acc_ref[...] * (1.0 / MOBA_BLOCK)).astype(jnp.bfloat16).astype(jnp.float32)
        q = q_ref[0].astype(jnp.bfloat16).astype(jnp.float32)
        g = jnp.sum(k_mean * q, axis=1)
        gate_ref[...] = jnp.where(lane == p // pages_per_block, g, gate_ref[...])

    @pl.when(p == n_pages - 1)
    def _():
        gate = gate_ref[...]
        rank = jnp.zeros(gate.shape, jnp.int32)
        for m in range(LANES):
            gm = gate[:, m:m + 1]
            beats = (gm > gate) | ((gm == gate) & (lane > m))
            rank = rank + beats.astype(jnp.int32)
        sel = jnp.zeros(gate.shape, jnp.int32)
        for r in range(MOBA_TOPK):
            idx = jnp.sum(jnp.where(rank == r, lane, 0), axis=1, keepdims=True)
            sel = jnp.where(lane == r, idx, sel)
        sel_ref[0] = sel


def _moba_sample_kernel(n_pages, layer, pt_ref, sel_ref, q_ref, kn_ref, vn_ref, ck_hbm, cv_hbm, o_ref,
                        kbuf, vbuf, sem):
    b = pl.program_id(0)
    pages_per_block = MOBA_BLOCK // PAGE_SIZE
    copies = []
    for h in range(MOBA_HEADS):
        for r in range(MOBA_TOPK):
            blk = sel_ref[(b * MOBA_HEADS + h) * MOBA_TOPK + r]
            for half in range(pages_per_block):
                page = pt_ref[b * n_pages + blk * pages_per_block + half]
                dst = pl.ds((r * pages_per_block + half) * PAGE_SIZE, PAGE_SIZE)
                copies.append(pltpu.make_async_copy(ck_hbm.at[layer, page, h], kbuf.at[h, :, dst], sem.at[0]))
                copies.append(pltpu.make_async_copy(cv_hbm.at[layer, page, h], vbuf.at[h, :, dst], sem.at[1]))
    for c in copies:
        c.start()
    for c in copies:
        c.wait()
    scale = HEAD_DIM ** -0.5
    nt = (((1,), (1,)), ((), ()))
    for h in range(MOBA_HEADS):
        q = q_ref[0, h:h + 1, :].astype(jnp.bfloat16)
        kn = kn_ref[0, h:h + 1, :].astype(jnp.bfloat16)
        vn = vn_ref[0, h:h + 1, :].astype(jnp.bfloat16)
        s_sel = jnp.dot(q, kbuf[h].astype(jnp.bfloat16), preferred_element_type=jnp.float32) * scale
        s_own = jnp.sum(q.astype(jnp.float32) * kn.astype(jnp.float32), axis=1, keepdims=True) * scale
        m = jnp.maximum(jnp.max(s_sel, axis=1, keepdims=True), s_own)
        p_sel = jnp.exp(s_sel - m)
        p_own = jnp.exp(s_own - m)
        denom = jnp.sum(p_sel, axis=1, keepdims=True) + p_own
        p_sel = (p_sel / denom).astype(jnp.bfloat16)
        p_own = (p_own / denom).astype(jnp.bfloat16).astype(jnp.float32)
        o = lax.dot_general(p_sel, vbuf[h].astype(jnp.bfloat16), nt, preferred_element_type=jnp.float32)
        o_ref[0, h:h + 1, :] = o + p_own * vn.astype(jnp.float32)


def moba_sample(q, k_new, v_new, cache_k, cache_v, layer, page_table):
    B, n_pages = page_table.shape
    assert (n_pages * PAGE_SIZE) % MOBA_BLOCK == 0 and MOBA_TOPK <= n_pages * PAGE_SIZE // MOBA_BLOCK <= LANES
    pt = page_table.reshape(B * n_pages)
    ck = cache_k.transpose(0, 1, 3, 4, 2)
    cv = cache_v.transpose(0, 1, 3, 4, 2)
    sel = pl.pallas_call(
        _moba_gate_kernel,
        grid_spec=pltpu.PrefetchScalarGridSpec(
            num_scalar_prefetch=1,
            grid=(B, n_pages),
            in_specs=[pl.BlockSpec((1, MOBA_HEADS, HEAD_DIM, 1), lambda b, p, pt_ref: (b, 0, 0, 0)),
                      pl.BlockSpec((1, 1, MOBA_HEADS, HEAD_DIM, PAGE_SIZE),
                                   lambda b, p, pt_ref: (layer, pt_ref[b * n_pages + p], 0, 0, 0))],
            out_specs=pl.BlockSpec((1, MOBA_HEADS, LANES), lambda b, p, pt_ref: (b, 0, 0)),
            scratch_shapes=[pltpu.VMEM((MOBA_HEADS, HEAD_DIM, 1), jnp.float32),
                            pltpu.VMEM((MOBA_HEADS, LANES), jnp.float32)]),
        out_shape=jax.ShapeDtypeStruct((B, MOBA_HEADS, LANES), jnp.int32),
        compiler_params=pltpu.CompilerParams(
            dimension_semantics=("parallel", "arbitrary"), vmem_limit_bytes=VMEM_LIMIT_BYTES),
        name="moba_gate",
    )(pt, q.reshape(B, MOBA_HEADS, HEAD_DIM, 1), ck)
    sel_flat = sel[:, :, :MOBA_TOPK].reshape(B * MOBA_HEADS * MOBA_TOPK)
    keys = MOBA_TOPK * MOBA_BLOCK
    vec = pl.BlockSpec((1, MOBA_HEADS, HEAD_DIM), lambda b, *_: (b, 0, 0))
    return pl.pallas_call(
        functools.partial(_moba_sample_kernel, n_pages, layer),
        grid_spec=pltpu.PrefetchScalarGridSpec(
            num_scalar_prefetch=2,
            grid=(B,),
            in_specs=[vec, vec, vec, pl.BlockSpec(memory_space=pl.ANY), pl.BlockSpec(memory_space=pl.ANY)],
            out_specs=vec,
            scratch_shapes=[pltpu.VMEM((MOBA_HEADS, HEAD_DIM, keys), jnp.float32),
                            pltpu.VMEM((MOBA_HEADS, HEAD_DIM, keys), jnp.float32),
                            pltpu.SemaphoreType.DMA((2,))]),
        out_shape=jax.ShapeDtypeStruct((B, MOBA_HEADS, HEAD_DIM), jnp.float32),
        compiler_params=pltpu.CompilerParams(
            dimension_semantics=("arbitrary",), vmem_limit_bytes=VMEM_LIMIT_BYTES),
        name="moba_sample",
    )(pt, sel_flat, q, k_new, v_new, ck, cv)


def layer_norm(x, g, b):
    xf = x.astype(jnp.float32)
    mu = jnp.mean(xf, -1, keepdims=True)
    var = jnp.mean(jnp.square(xf - mu), -1, keepdims=True)
    return ((xf - mu) * lax.rsqrt(var + LN_EPS)).astype(x.dtype) * g + b


def rope(x, pos):
    half = x.shape[-1] // 2
    inv = ROPE_THETA ** (-jnp.arange(half, dtype=jnp.float32) / half)
    ang = pos.astype(jnp.float32)[:, None] * inv[None, :]
    cos = jnp.cos(ang)[None, :, None, :]
    sin = jnp.sin(ang)[None, :, None, :]
    x1 = x[..., :half].astype(jnp.float32)
    x2 = x[..., half:].astype(jnp.float32)
    return jnp.concatenate([x1 * cos - x2 * sin, x2 * cos + x1 * sin], -1).astype(x.dtype)


WKV_TIME_CHUNK = 32


def _wkv_scan_kernel(w_ref, nkk_ref, kka_ref, k_ref, r_ref, v_ref, s0_ref, y_ref, sT_ref, s_ref):
    tc = pl.program_id(1)
    steps = w_ref.shape[0]
    nvi = s_ref.shape[0]

    @pl.when(tc == 0)
    def _():
        s_ref[...] = s0_ref[...]

    def step(t, carry):
        w = w_ref[t]
        nkk = nkk_ref[t]
        kka = kka_ref[t]
        kv = k_ref[t]
        r = r_ref[t]
        for vi in range(nvi):
            s = s_ref[vi]
            sa = jnp.sum(s * nkk, axis=0, keepdims=True)
            s_new = s * w + sa * kka + v_ref[t, vi:vi + 1, :] * kv
            s_ref[vi] = s_new
            y_ref[t, vi:vi + 1, :] = jnp.sum(s_new * r, axis=0, keepdims=True)
        return carry

    lax.fori_loop(0, steps, step, 0)

    @pl.when(tc == pl.num_programs(1) - 1)
    def _():
        sT_ref[...] = s_ref[...]


def wkv_scan_pallas(r, decay, k, v, kk, a, s0):
    B, T, H, N = r.shape
    bh = B * H
    vh = max(1, LANES // bh)
    nvi = N // vh
    width = vh * bh
    assert width % LANES == 0
    tcs = min(WKV_TIME_CHUNK, T)
    assert T % tcs == 0

    def key_major(x):
        xt = jnp.transpose(x, (1, 3, 0, 2)).reshape(T, N, bh)
        return jnp.concatenate([xt] * vh, axis=-1)

    v_l = jnp.transpose(v.reshape(B, T, H, vh, nvi), (1, 4, 3, 0, 2)).reshape(T, nvi, width)
    s0_l = jnp.transpose(s0.reshape(B, H, vh, nvi, N), (3, 4, 2, 0, 1)).reshape(nvi, N, width)
    kspec = pl.BlockSpec((tcs, N, LANES), lambda j, t: (t, 0, j))
    vspec = pl.BlockSpec((tcs, nvi, LANES), lambda j, t: (t, 0, j))
    sspec = pl.BlockSpec((nvi, N, LANES), lambda j, t: (0, 0, j))
    y_l, sT_l = pl.pallas_call(
        _wkv_scan_kernel,
        grid=(width // LANES, T // tcs),
        in_specs=[kspec, kspec, kspec, kspec, kspec, vspec, sspec],
        out_specs=[vspec, sspec],
        out_shape=[jax.ShapeDtypeStruct((T, nvi, width), jnp.float32),
                   jax.ShapeDtypeStruct((nvi, N, width), jnp.float32)],
        scratch_shapes=[pltpu.VMEM((nvi, N, LANES), jnp.float32)],
        compiler_params=pltpu.CompilerParams(
            dimension_semantics=("parallel", "arbitrary"), vmem_limit_bytes=VMEM_LIMIT_BYTES),
        name="wkv_scan",
    )(key_major(decay), key_major(-kk), key_major(kk * a), key_major(k), key_major(r), v_l, s0_l)
    y = jnp.transpose(y_l.reshape(T, nvi, vh, B, H), (3, 0, 4, 2, 1)).reshape(B, T, H, N)
    sT = jnp.transpose(sT_l.reshape(nvi, N, vh, B, H), (3, 4, 2, 0, 1)).reshape(B, H, N, N)
    return y, sT


def rwkv_group(proj, prev, s0, mu, w0, w2, a0, a2, g2, k_k, k_a, r_k, lnx_g, lnx_b):
    B, T, _ = proj.shape
    shifted = jnp.concatenate([prev[:, None, :].astype(proj.dtype), proj[:, :-1]], axis=1)
    xm = proj + mu * (shifted - proj)
    c1, c2, c3 = RWKV_WIDTH, 2 * RWKV_WIDTH, 3 * RWKV_WIDTH
    c4 = c3 + LORA_W
    c5 = c4 + LORA_A
    r, k, v = xm[..., :c1], xm[..., c1:c2], xm[..., c2:c3]
    xw, xa, xg = xm[..., c3:c4], xm[..., c4:c5], xm[..., c5:]
    w = -jax.nn.softplus(-(w0 + jnp.tanh(xw) @ w2)) - 0.5
    decay = jnp.exp(-jnp.exp(w.astype(jnp.float32)))
    a = jax.nn.sigmoid(a0 + xa @ a2)
    g = jax.nn.sigmoid(xg) @ g2
    heads = lambda t: t.astype(jnp.float32).reshape(B, T, RWKV_HEADS, HEAD_DIM)
    kk = heads(k * k_k)
    kk = kk / jnp.maximum(jnp.sqrt(jnp.sum(kk * kk, -1, keepdims=True)), 1e-12)
    k = k * (1.0 + (a - 1.0) * k_a)
    r_h, k_h, v_h, a_h, w_h = heads(r), heads(k), heads(v), heads(a), heads(decay)
    y, s = wkv_scan_pallas(r_h, w_h, k_h, v_h, kk, a_h, s0.astype(jnp.float32))
    mean = jnp.mean(y, -1, keepdims=True)
    var = jnp.mean(jnp.square(y - mean), -1, keepdims=True)
    y = ((y - mean) * lax.rsqrt(var + GN_EPS)).reshape(B, T, RWKV_WIDTH) * lnx_g + lnx_b
    bonus = (jnp.sum(r_h * k_h * r_k, -1, keepdims=True) * v_h).reshape(B, T, RWKV_WIDTH)
    out = ((y + bonus) * g).astype(proj.dtype)
    return out, s.astype(s0.dtype), proj[:, -1]


def mem_attend(x, mk, mv, wq, wo):
    B, T, _ = x.shape
    q = pmatmul(x.reshape(B * T, D_MODEL), wq).reshape(B, T, MEM_HEADS, MEM_HEAD_DIM)
    s = jnp.einsum('bthd,bmhd->bhtm', q, mk.astype(q.dtype), preferred_element_type=jnp.float32) * MEM_HEAD_DIM ** -0.5
    prob = jax.nn.softmax(s, axis=-1).astype(x.dtype)
    o = jnp.einsum('bhtm,bmhd->bthd', prob, mv.astype(x.dtype)).reshape(B * T, D_MODEL)
    return pmatmul(o, wo).reshape(B, T, D_MODEL)


def run_trunk(x, pos0, shift0, wkv0, cache_k, cache_v, page_table, mem_k, mem_v, p):
    B, T, _ = x.shape
    pos = pos0 + jnp.arange(T, dtype=jnp.int32)
    k_rows, v_rows, wkvs, shifts = [], [], [], []
    for i in range(DEPTH):
        proj = pmatmul(x.reshape(B * T, D_MODEL), p['w_in'][i]).reshape(B, T, IN_PROJ)
        rw_out, wkv_i, shift_i = rwkv_group(
            proj[..., :RWKV_PROJ], shift0[i], wkv0[i], p['shift_mu'][i],
            p['rwkv_w0'][i], p['rwkv_w2'][i], p['rwkv_a0'][i], p['rwkv_a2'][i],
            p['rwkv_g2'][i], p['rwkv_k_k'][i], p['rwkv_k_a'][i], p['rwkv_r_k'][i],
            p['rwkv_lnx_g'][i], p['rwkv_lnx_b'][i])
        q, k, v = jnp.split(proj[..., RWKV_PROJ:], 3, axis=-1)
        q = rope(q.reshape(B, T, MOBA_HEADS, HEAD_DIM), pos)
        k = rope(k.reshape(B, T, MOBA_HEADS, HEAD_DIM), pos)
        v = v.reshape(B, T, MOBA_HEADS, HEAD_DIM)
        if cache_k is None:
            mb_out = moba_prompt(q.reshape(B, T, MOBA_WIDTH), k.reshape(B, T, MOBA_WIDTH),
                                 v.reshape(B, T, MOBA_WIDTH))
        else:
            assert T == 1 and pos0 % MOBA_BLOCK == 0
            mb_out = moba_sample(q[:, 0], k[:, 0], v[:, 0], cache_k, cache_v, i,
                                 page_table).reshape(B, T, MOBA_WIDTH)
        mix = pmatmul(jnp.concatenate([rw_out, mb_out], axis=-1).reshape(B * T, D_MODEL),
                      p['w_out'][i]).reshape(B, T, D_MODEL)
        x = layer_norm(DEEPNORM_ALPHA * x + mix, p['ln1_g'][i], p['ln1_b'][i])
        cross = mem_attend(x, mem_k[i], mem_v[i], p['w_mem_q'][i], p['w_mem_o'][i])
        x = layer_norm(DEEPNORM_ALPHA * x + cross, p['ln2_g'][i], p['ln2_b'][i])
        x = peer_ffn_ln(x, p['peer_w_q_bf16'][i], p['peer_sub_keys_bf16'][i], p['peer_uv'][i],
                        p['ln3_g'][i], p['ln3_b'][i])
        k_rows.append(k)
        v_rows.append(v)
        wkvs.append(wkv_i)
        shifts.append(shift_i)
    return x, jnp.stack(k_rows), jnp.stack(v_rows), jnp.stack(wkvs), jnp.stack(shifts)


def kernel(x_prompt, x_sample, mem_prompt, cache_k, cache_v, page_table, state_wkv,
           state_shift, cache_mem_k, cache_mem_v, w_in, shift_mu, rwkv_w0, rwkv_w2,
           rwkv_a0, rwkv_a2, rwkv_g2, rwkv_k_k, rwkv_k_a, rwkv_r_k, rwkv_lnx_g,
           rwkv_lnx_b, w_out, ln1_g, ln1_b, w_mem_q, w_mem_kv, w_mem_o, ln2_g, ln2_b,
           peer_w_q, peer_sub_keys, peer_u, peer_v, ln3_g, ln3_b):
    p = dict(w_in=w_in, shift_mu=shift_mu, rwkv_w0=rwkv_w0, rwkv_w2=rwkv_w2,
             rwkv_a0=rwkv_a0, rwkv_a2=rwkv_a2, rwkv_g2=rwkv_g2, rwkv_k_k=rwkv_k_k,
             rwkv_k_a=rwkv_k_a, rwkv_r_k=rwkv_r_k, rwkv_lnx_g=rwkv_lnx_g,
             rwkv_lnx_b=rwkv_lnx_b, w_out=w_out, ln1_g=ln1_g, ln1_b=ln1_b,
             w_mem_q=w_mem_q, w_mem_o=w_mem_o, ln2_g=ln2_g, ln2_b=ln2_b,
             ln3_g=ln3_g, ln3_b=ln3_b)
    p['peer_uv'] = jnp.concatenate([peer_u, peer_v], axis=-1)
    p['peer_w_q_bf16'] = peer_w_q.astype(jnp.bfloat16)
    p['peer_sub_keys_bf16'] = peer_sub_keys.astype(jnp.bfloat16)
    B = x_prompt.shape[0]
    mem_len = mem_prompt.shape[1]
    mkv = jnp.stack([pmatmul(mem_prompt.reshape(B * mem_len, D_MODEL), w_mem_kv[i]).reshape(B, mem_len, 2 * D_MODEL)
                     for i in range(DEPTH)])
    mem_k_prompt = mkv[..., :D_MODEL].reshape(DEPTH, B, mem_len, MEM_HEADS, MEM_HEAD_DIM)
    mem_v_prompt = mkv[..., D_MODEL:].reshape(DEPTH, B, mem_len, MEM_HEADS, MEM_HEAD_DIM)
    shift0 = jnp.zeros((DEPTH, B, RWKV_PROJ), x_prompt.dtype)
    wkv0 = jnp.zeros((DEPTH, B, RWKV_HEADS, HEAD_DIM, HEAD_DIM), x_prompt.dtype)
    past_len = page_table.shape[1] * PAGE_SIZE
    y_sample, k_sample, v_sample, wkv_sample, shift_sample = run_trunk(
        x_sample, past_len, state_shift, state_wkv, cache_k, cache_v, page_table,
        cache_mem_k, cache_mem_v, p)
    y_prompt, k_prompt, v_prompt, wkv_prompt, shift_prompt = run_trunk(
        x_prompt, 0, shift0, wkv0, None, None, None, mem_k_prompt, mem_v_prompt, p)
    return (y_prompt, y_sample, k_prompt, v_prompt, wkv_prompt, shift_prompt,
            mem_k_prompt, mem_v_prompt, k_sample, v_sample, wkv_sample, shift_sample)
```

```python
import functools

import jax
import jax.numpy as jnp
from jax import lax
from jax.experimental import pallas as pl
from jax.experimental.pallas import tpu as pltpu

D_MODEL = 1024
DEPTH = 2
PAGE_SIZE = 128
HEAD_DIM = 64
RWKV_WIDTH = D_MODEL // 2
MOBA_WIDTH = D_MODEL - RWKV_WIDTH
RWKV_HEADS = RWKV_WIDTH // HEAD_DIM
MOBA_HEADS = MOBA_WIDTH // HEAD_DIM
LORA_W = max(32, int(round(1.8 * D_MODEL ** 0.5 / 32)) * 32)
LORA_A = max(32, int(round(1.8 * D_MODEL ** 0.5 / 32)) * 32)
LORA_G = max(32, int(round(0.6 * D_MODEL ** 0.8 / 32)) * 32)
RWKV_PROJ = 3 * RWKV_WIDTH + LORA_W + LORA_A + LORA_G
MOBA_PROJ = 3 * MOBA_WIDTH
IN_PROJ = RWKV_PROJ + MOBA_PROJ
GN_EPS = 64e-5
MOBA_BLOCK = 256
MOBA_TOPK = 3
ROPE_THETA = 10000.0
NEG_INF = -1e30
MEM_HEADS = 4
MEM_HEAD_DIM = D_MODEL // MEM_HEADS
PEER_HEADS = 8
PEER_NKEYS = 128
PEER_TOPK = 16
PEER_DKEY = 256
PEER_HALF = PEER_DKEY // 2
DEEPNORM_ALPHA = (2 * DEPTH) ** 0.25
LN_EPS = 1e-5

VMEM_LIMIT_BYTES = 56 * 1024 * 1024
SUBLANES = 8
LANES = 128


def _matmul_kernel(x_ref, w_ref, o_ref):
    o_ref[...] = jnp.dot(x_ref[...].astype(jnp.bfloat16), w_ref[...].astype(jnp.bfloat16),
                         preferred_element_type=jnp.float32)


def pmatmul(x, w, tm=512, tn=512):
    m, k = x.shape
    n = w.shape[1]
    tm = min(tm, m)
    assert m % tm == 0
    if n % tn != 0:
        tn = n
    return pl.pallas_call(
        _matmul_kernel,
        grid=(m // tm, n // tn),
        in_specs=[pl.BlockSpec((tm, k), lambda i, j: (i, 0)),
                  pl.BlockSpec((k, tn), lambda i, j: (0, j))],
        out_specs=pl.BlockSpec((tm, tn), lambda i, j: (i, j)),
        out_shape=jax.ShapeDtypeStruct((m, n), jnp.float32),
        compiler_params=pltpu.CompilerParams(
            dimension_semantics=("parallel", "parallel"),
            vmem_limit_bytes=VMEM_LIMIT_BYTES),
    )(x, w)


PEER_ROUTE_TOKENS = 256
PEER_GATHER_TOKENS = 8
PEER_SLOTS = PEER_HEADS * PEER_TOPK


def _topk_sublane(s, k):
    n_rows = s.shape[0]
    iota = lax.broadcasted_iota(jnp.int32, s.shape, 0)
    vals, idxs = [], []
    for _ in range(k):
        m = jnp.max(s, axis=0, keepdims=True)
        idx = jnp.min(jnp.where(s == m, iota, n_rows), axis=0, keepdims=True)
        vals.append(m)
        idxs.append(idx)
        s = jnp.where(iota == idx, -jnp.inf, s)
    return vals, idxs


def _peer_route_kernel(x_ref, wq_ref, sk_ref, e_ref, g_ref):
    q = jnp.dot(x_ref[...].astype(jnp.bfloat16), wq_ref[...], preferred_element_type=jnp.float32)
    nt = (((1,), (1,)), ((), ()))
    for h in range(PEER_HEADS):
        base = h * PEER_DKEY
        q1 = q[:, base:base + PEER_HALF].astype(jnp.bfloat16)
        q2 = q[:, base + PEER_HALF:base + PEER_DKEY].astype(jnp.bfloat16)
        s1 = lax.dot_general(sk_ref[0], q1, nt, preferred_element_type=jnp.float32)
        s2 = lax.dot_general(sk_ref[1], q2, nt, preferred_element_type=jnp.float32)
        v1_rows, i1_rows = _topk_sublane(s1, PEER_TOPK)
        v2_rows, i2_rows = _topk_sublane(s2, PEER_TOPK)
        v2 = jnp.concatenate(v2_rows, axis=0)
        i2 = jnp.concatenate(i2_rows, axis=0)
        tt = s1.shape[1]
        v2s = [v2] + [jnp.concatenate(v2_rows[:SUBLANES], axis=0)] * (PEER_TOPK - 1)
        i2s = [i2] + [jnp.concatenate(i2_rows[:SUBLANES], axis=0)] * (PEER_TOPK - 1)
        subs = ([lax.broadcasted_iota(jnp.int32, (PEER_TOPK, tt), 0)]
                + [lax.broadcasted_iota(jnp.int32, (SUBLANES, tt), 0)] * (PEER_TOPK - 1))
        cand = jnp.concatenate([v1_rows[a] + v2s[a] for a in range(PEER_TOPK)], axis=0)
        cand_idx = jnp.concatenate([i1_rows[a] * PEER_NKEYS + i2s[a] for a in range(PEER_TOPK)], axis=0)
        pos_all = jnp.concatenate([a * PEER_TOPK + subs[a] for a in range(PEER_TOPK)], axis=0)
        n_cand = PEER_TOPK * PEER_TOPK
        top_s, top_e = [], []
        for _ in range(PEER_TOPK):
            m = jnp.max(cand, axis=0, keepdims=True)
            pos = jnp.min(jnp.where(cand == m, pos_all, n_cand), axis=0, keepdims=True)
            hit = pos_all == pos
            top_s.append(m)
            top_e.append(jnp.sum(jnp.where(hit, cand_idx, 0), axis=0, keepdims=True))
            cand = jnp.where(hit, -jnp.inf, cand)
        top_s = jnp.concatenate(top_s, axis=0)
        p = jnp.exp(top_s - top_s[0:1, :])
        g = p / jnp.sum(p, axis=0, keepdims=True)
        e_ref[h * PEER_TOPK:(h + 1) * PEER_TOPK, :] = jnp.concatenate(top_e, axis=0)
        g_ref[h * PEER_TOPK:(h + 1) * PEER_TOPK, :] = g


def peer_route(x, wq_bf16, sk_bf16):
    n = x.shape[0]
    tt = min(PEER_ROUTE_TOKENS, n)
    assert n % tt == 0
    return pl.pallas_call(
        _peer_route_kernel,
        grid=(n // tt,),
        in_specs=[pl.BlockSpec((tt, D_MODEL), lambda i: (i, 0)),
                  pl.BlockSpec((D_MODEL, PEER_HEADS * PEER_DKEY), lambda i: (0, 0)),
                  pl.BlockSpec((2, PEER_NKEYS, PEER_HALF), lambda i: (0, 0, 0))],
        out_specs=[pl.BlockSpec((PEER_SLOTS, tt), lambda i: (0, i)),
                   pl.BlockSpec((PEER_SLOTS, tt), lambda i: (0, i))],
        out_shape=[jax.ShapeDtypeStruct((PEER_SLOTS, n), jnp.int32),
                   jax.ShapeDtypeStruct((PEER_SLOTS, n), jnp.float32)],
        compiler_params=pltpu.CompilerParams(
            dimension_semantics=("parallel",), vmem_limit_bytes=VMEM_LIMIT_BYTES),
        name="peer_route",
    )(x, wq_bf16, sk_bf16)


def _peer_gather_kernel(e_hbm, x_ref, g_ref, lng_ref, lnb_ref, uv_hbm, o_ref,
                        idx_smem, buf, sem_idx, sem_rows):
    i = pl.program_id(0)
    n = pl.num_programs(0)
    rows = PEER_GATHER_TOKENS * PEER_SLOTS
    groups = PEER_SLOTS // SUBLANES
    half = D_MODEL // LANES

    def idx_copy(tile, slot):
        return pltpu.make_async_copy(e_hbm.at[pl.ds(tile * rows, rows)], idx_smem.at[slot], sem_idx.at[slot])

    def row_copy(row, slot, r_hi, r_lo):
        return pltpu.make_async_copy(uv_hbm.at[row], buf.at[slot, r_hi, :, r_lo, :], sem_rows.at[slot])

    def rows_wait(slot):
        pltpu.make_async_copy(uv_hbm.at[pl.ds(0, rows)], buf.at[slot], sem_rows.at[slot]).wait()

    last = n - 1

    @pl.when(i == 0)
    def _():
        idx_copy(0, 0).start()
        idx_copy(0, 0).wait()

        def body(r, carry):
            row_copy(idx_smem[0, r], 0, r // SUBLANES, r % SUBLANES).start()
            return carry
        lax.fori_loop(0, rows, body, 0)
        idx_copy(jnp.minimum(1, last), 1).start()

    per_phase = PEER_SLOTS // (2 * half)

    def step(cur, nxt):
        idx_copy(0, nxt).wait()
        idx_copy(jnp.minimum(i + 2, last), cur).start()
        rows_wait(cur)
        g_tile = g_ref[0]
        for t in range(PEER_GATHER_TOKENS):
            def prefetch(phase):
                for r in range(t * PEER_SLOTS + phase * per_phase, t * PEER_SLOTS + (phase + 1) * per_phase):
                    row_copy(idx_smem[nxt, r], nxt, r // SUBLANES, r % SUBLANES).start(priority=r % 2)
            x_t = x_ref[t]
            lo = t * groups
            acc = None
            for lg in range(half):
                prefetch(lg)
                term = buf[cur, lo:lo + groups, lg] * x_t[lg:lg + 1, :]
                acc = term if acc is None else acc + term
            hid = jnp.sum(acc, axis=-1, keepdims=True)
            act = 0.5 * hid * (1.0 + lax.erf(hid * (2.0 ** -0.5)))
            wgt = g_tile[:, t:t + 1].reshape(groups, SUBLANES, 1) * act
            out_rows = []
            for lg in range(half):
                prefetch(half + lg)
                part = jnp.sum(buf[cur, lo:lo + groups, half + lg] * wgt, axis=0)
                out_rows.append(jnp.sum(part, axis=0, keepdims=True))
            z = DEEPNORM_ALPHA * x_t + jnp.concatenate(out_rows, axis=0)
            mu = jnp.sum(jnp.sum(z, axis=1, keepdims=True), axis=0, keepdims=True) * (1.0 / D_MODEL)
            zc = z - mu
            var = jnp.sum(jnp.sum(zc * zc, axis=1, keepdims=True), axis=0, keepdims=True) * (1.0 / D_MODEL)
            o_ref[t] = zc * lax.rsqrt(var + LN_EPS) * lng_ref[...] + lnb_ref[...]

        @pl.when(i == last)
        def _():
            rows_wait(nxt)
            idx_copy(0, cur).wait()

    @pl.when(i % 2 == 0)
    def _():
        step(0, 1)

    @pl.when(i % 2 == 1)
    def _():
        step(1, 0)


def peer_gather_ln(x, e, g, uv, ln_g, ln_b):
    n = x.shape[0]
    tb = PEER_GATHER_TOKENS
    assert n % tb == 0 and n // tb >= 2
    rows = tb * PEER_SLOTS
    lane_groups = D_MODEL // LANES
    g3 = g.reshape(n // tb, tb, PEER_SLOTS).transpose(0, 2, 1)
    y = pl.pallas_call(
        _peer_gather_kernel,
        grid=(n // tb,),
        in_specs=[pl.BlockSpec(memory_space=pl.ANY),
                  pl.BlockSpec((tb, lane_groups, LANES), lambda i: (i, 0, 0)),
                  pl.BlockSpec((1, PEER_SLOTS, tb), lambda i: (i, 0, 0)),
                  pl.BlockSpec((lane_groups, LANES), lambda i: (0, 0)),
                  pl.BlockSpec((lane_groups, LANES), lambda i: (0, 0)),
                  pl.BlockSpec(memory_space=pl.ANY)],
        out_specs=pl.BlockSpec((tb, lane_groups, LANES), lambda i: (i, 0, 0)),
        out_shape=jax.ShapeDtypeStruct((n, lane_groups, LANES), jnp.float32),
        scratch_shapes=[pltpu.SMEM((2, rows), jnp.int32),
                        pltpu.VMEM((2, rows // SUBLANES, 2 * lane_groups, SUBLANES, LANES), jnp.float32),
                        pltpu.SemaphoreType.DMA((2,)),
                        pltpu.SemaphoreType.DMA((2,))],
        compiler_params=pltpu.CompilerParams(
            dimension_semantics=("arbitrary",), vmem_limit_bytes=VMEM_LIMIT_BYTES),
        name="peer_gather",
    )(e.reshape(n * PEER_SLOTS), x.reshape(n, lane_groups, LANES), g3,
      ln_g.reshape(lane_groups, LANES), ln_b.reshape(lane_groups, LANES),
      uv.reshape(uv.shape[0], 2 * lane_groups, LANES))
    return y.reshape(n, D_MODEL)


def peer_ffn_ln(x, wq_bf16, sk_bf16, uv, ln_g, ln_b):
    B, T, D = x.shape
    xf = x.reshape(B * T, D)
    e_t, g_t = peer_route(xf, wq_bf16, sk_bf16)
    y = peer_gather_ln(xf, e_t.T, g_t.T, uv, ln_g, ln_b)
    return y.reshape(B, T, D)


def _moba_prompt_kernel(qt_ref, k_ref, vt_ref, o_ref, kmean_ref, sel_ref):
    qi = pl.program_id(2)
    nb = k_ref.shape[2]
    blk = k_ref.shape[3]
    scale = HEAD_DIM ** -0.5

    @pl.when(qi == 0)
    def _():
        for nblk in range(nb):
            kmean_ref[nblk:nblk + 1, :] = jnp.sum(k_ref[0, 0, nblk], axis=0, keepdims=True) * (1.0 / blk)

    qt = qt_ref[0, 0, 0].astype(jnp.bfloat16)
    gate = jnp.dot(kmean_ref[...].astype(jnp.bfloat16), qt, preferred_element_type=jnp.float32)
    blk_id = lax.broadcasted_iota(jnp.int32, gate.shape, 0)
    gate = jnp.where(blk_id < qi, gate, NEG_INF)
    rank = jnp.zeros(gate.shape, jnp.int32)
    for m in range(nb):
        gm = gate[m:m + 1, :]
        beats = (gm > gate) | ((gm == gate) & (blk_id > m))
        rank = rank + beats.astype(jnp.int32)
    sel_ref[...] = ((rank < MOBA_TOPK) & (blk_id < qi)).astype(jnp.float32)

    key_pos = lax.broadcasted_iota(jnp.int32, (blk, blk), 0)
    qry_pos = lax.broadcasted_iota(jnp.int32, (blk, blk), 1)
    causal = key_pos <= qry_pos

    def attend_pair(j, carry):
        m_run, l_run, acc = carry
        ss, keeps = [], []
        for d in range(2):
            n = 2 * j + d
            keep = (sel_ref[pl.ds(n, 1), :] > 0.5) | jnp.logical_and(n == qi, causal)
            s = jnp.dot(k_ref[0, 0, n].astype(jnp.bfloat16), qt, preferred_element_type=jnp.float32) * scale
            ss.append(jnp.where(keep, s, NEG_INF))
            keeps.append(keep)
        m_new = jnp.maximum(m_run, jnp.maximum(jnp.max(ss[0], axis=0, keepdims=True),
                                               jnp.max(ss[1], axis=0, keepdims=True)))
        alpha = jnp.exp(m_run - m_new)
        l_new = alpha * l_run
        acc_new = alpha * acc
        for d in range(2):
            p = jnp.where(keeps[d], jnp.exp(ss[d] - m_new), 0.0)
            l_new = l_new + jnp.sum(p, axis=0, keepdims=True)
            acc_new = acc_new + jnp.dot(vt_ref[0, 0, 2 * j + d].astype(jnp.bfloat16), p.astype(jnp.bfloat16),
                                        preferred_element_type=jnp.float32)
        return m_new, l_new, acc_new

    init = (jnp.full((1, blk), NEG_INF, jnp.float32), jnp.zeros((1, blk), jnp.float32),
            jnp.zeros((HEAD_DIM, blk), jnp.float32))
    m_run, l_run, acc = lax.fori_loop(0, (qi + 2) // 2, attend_pair, init)
    o_ref[0, 0, 0] = acc / l_run


def moba_prompt(q, k, v):
    B, T, _ = q.shape
    nb = T // MOBA_BLOCK
    assert T % MOBA_BLOCK == 0 and nb % 2 == 0
    split = lambda t: t.reshape(B, nb, MOBA_BLOCK, MOBA_HEADS, HEAD_DIM)
    qt = split(q).transpose(0, 3, 1, 4, 2)
    kb = split(k).transpose(0, 3, 1, 2, 4)
    vt = split(v).transpose(0, 3, 1, 4, 2)
    ot = pl.pallas_call(
        _moba_prompt_kernel,
        grid=(B, MOBA_HEADS, nb),
        in_specs=[pl.BlockSpec((1, 1, 1, HEAD_DIM, MOBA_BLOCK), lambda b, h, i: (b, h, i, 0, 0)),
                  pl.BlockSpec((1, 1, nb, MOBA_BLOCK, HEAD_DIM), lambda b, h, i: (b, h, 0, 0, 0)),
                  pl.BlockSpec((1, 1, nb, HEAD_DIM, MOBA_BLOCK), lambda b, h, i: (b, h, 0, 0, 0))],
        out_specs=pl.BlockSpec((1, 1, 1, HEAD_DIM, MOBA_BLOCK), lambda b, h, i: (b, h, i, 0, 0)),
        out_shape=jax.ShapeDtypeStruct((B, MOBA_HEADS, nb, HEAD_DIM, MOBA_BLOCK), jnp.float32),
        scratch_shapes=[pltpu.VMEM((nb, HEAD_DIM), jnp.float32),
                        pltpu.VMEM((nb, MOBA_BLOCK), jnp.float32)],
        compiler_params=pltpu.CompilerParams(
            dimension_semantics=("parallel", "parallel", "arbitrary"), vmem_limit_bytes=VMEM_LIMIT_BYTES),
        name="moba_prompt",
    )(qt, kb, vt)
    return ot.transpose(0, 2, 4, 1, 3).reshape(B, T, MOBA_HEADS * HEAD_DIM)


MOBA_GATE_PAGES = 8


def _moba_gate_kernel(pt_ref, q_ref, *refs):
    del pt_ref
    page_refs, sel_ref, gate_ref = refs[:MOBA_GATE_PAGES], refs[MOBA_GATE_PAGES], refs[MOBA_GATE_PAGES + 1]
    p = pl.program_id(1)
    pages_per_block = MOBA_BLOCK // PAGE_SIZE
    blocks_per_step = MOBA_GATE_PAGES // pages_per_block
    lane = lax.broadcasted_iota(jnp.int32, (MOBA_HEADS, LANES), 1)

    @pl.when(p == 0)
    def _():
        gate_ref[...] = jnp.full(gate_ref.shape, NEG_INF, jnp.float32)

    q = q_ref[0].astype(jnp.bfloat16).astype(jnp.float32)
    gate = gate_ref[...]
    for j in range(blocks_per_step):
        block = page_refs[j * pages_per_block][0, 0]
        for h in range(1, pages_per_block):
            block = block + page_refs[j * pages_per_block + h][0, 0]
        k_mean = (jnp.sum(block, axis=-1, keepdims=True) * (1.0 / MOBA_BLOCK)).astype(jnp.bfloat16)
        g = jnp.sum(k_mean.astype(jnp.float32) * q, axis=1)
        gate = jnp.where(lane == p * blocks_per_step + j, g, gate)
    gate_ref[...] = gate

    @pl.when(p == pl.num_programs(1) - 1)
    def _():
        rank = jnp.zeros(gate.shape, jnp.int32)
        for m in range(LANES):
            gm = gate[:, m:m + 1]
            beats = (gm > gate) | ((gm == gate) & (lane > m))
            rank = rank + beats.astype(jnp.int32)
        sel = jnp.zeros(gate.shape, jnp.int32)
        for r in range(MOBA_TOPK):
            idx = jnp.sum(jnp.where(rank == r, lane, 0), axis=1, keepdims=True)
            sel = jnp.where(lane == r, idx, sel)
        sel_ref[0] = sel


def _moba_sample_kernel(n_pages, layer, pt_ref, sel_ref, q_ref, kn_ref, vn_ref, ck_hbm, cv_hbm, o_ref,
                        kbuf, vbuf, sem):
    b = pl.program_id(0)
    pages_per_block = MOBA_BLOCK // PAGE_SIZE
    copies = []
    for h in range(MOBA_HEADS):
        for r in range(MOBA_TOPK):
            blk = sel_ref[(b * MOBA_HEADS + h) * MOBA_TOPK + r]
            for half in range(pages_per_block):
                page = pt_ref[b * n_pages + blk * pages_per_block + half]
                dst = pl.ds((r * pages_per_block + half) * PAGE_SIZE, PAGE_SIZE)
                copies.append(pltpu.make_async_copy(ck_hbm.at[layer, page, h], kbuf.at[h, :, dst], sem.at[0]))
                copies.append(pltpu.make_async_copy(cv_hbm.at[layer, page, h], vbuf.at[h, :, dst], sem.at[1]))
    for c in copies:
        c.start()
    for c in copies:
        c.wait()
    scale = HEAD_DIM ** -0.5
    nt = (((1,), (1,)), ((), ()))
    for h in range(MOBA_HEADS):
        q = q_ref[0, h:h + 1, :].astype(jnp.bfloat16)
        kn = kn_ref[0, h:h + 1, :].astype(jnp.bfloat16)
        vn = vn_ref[0, h:h + 1, :].astype(jnp.bfloat16)
        s_sel = jnp.dot(q, kbuf[h].astype(jnp.bfloat16), preferred_element_type=jnp.float32) * scale
        s_own = jnp.sum(q.astype(jnp.float32) * kn.astype(jnp.float32), axis=1, keepdims=True) * scale
        m = jnp.maximum(jnp.max(s_sel, axis=1, keepdims=True), s_own)
        p_sel = jnp.exp(s_sel - m)
        p_own = jnp.exp(s_own - m)
        denom = jnp.sum(p_sel, axis=1, keepdims=True) + p_own
        p_sel = (p_sel / denom).astype(jnp.bfloat16)
        p_own = (p_own / denom).astype(jnp.bfloat16).astype(jnp.float32)
        o = lax.dot_general(p_sel, vbuf[h].astype(jnp.bfloat16), nt, preferred_element_type=jnp.float32)
        o_ref[0, h:h + 1, :] = o + p_own * vn.astype(jnp.float32)


def moba_sample(q, k_new, v_new, cache_k, cache_v, layer, page_table):
    B, n_pages = page_table.shape
    assert n_pages % MOBA_GATE_PAGES == 0 and MOBA_GATE_PAGES % (MOBA_BLOCK // PAGE_SIZE) == 0
    assert MOBA_TOPK <= n_pages * PAGE_SIZE // MOBA_BLOCK <= LANES
    pt = page_table.reshape(B * n_pages)
    ck = cache_k.transpose(0, 1, 3, 4, 2)
    cv = cache_v.transpose(0, 1, 3, 4, 2)
    steps = n_pages // MOBA_GATE_PAGES

    def page_spec(j):
        return pl.BlockSpec((1, 1, MOBA_HEADS, HEAD_DIM, PAGE_SIZE),
                            lambda b, p, pt_ref: (layer, pt_ref[b * n_pages + p * MOBA_GATE_PAGES + j], 0, 0, 0))

    sel = pl.pallas_call(
        _moba_gate_kernel,
        grid_spec=pltpu.PrefetchScalarGridSpec(
            num_scalar_prefetch=1,
            grid=(B, steps),
            in_specs=[pl.BlockSpec((1, MOBA_HEADS, HEAD_DIM, 1), lambda b, p, pt_ref: (b, 0, 0, 0))]
                     + [page_spec(j) for j in range(MOBA_GATE_PAGES)],
            out_specs=pl.BlockSpec((1, MOBA_HEADS, LANES), lambda b, p, pt_ref: (b, 0, 0)),
            scratch_shapes=[pltpu.VMEM((MOBA_HEADS, LANES), jnp.float32)]),
        out_shape=jax.ShapeDtypeStruct((B, MOBA_HEADS, LANES), jnp.int32),
        compiler_params=pltpu.CompilerParams(
            dimension_semantics=("parallel", "arbitrary"), vmem_limit_bytes=VMEM_LIMIT_BYTES),
        name="moba_gate",
    )(pt, q.reshape(B, MOBA_HEADS, HEAD_DIM, 1), *([ck] * MOBA_GATE_PAGES))
    sel_flat = sel[:, :, :MOBA_TOPK].reshape(B * MOBA_HEADS * MOBA_TOPK)
    keys = MOBA_TOPK * MOBA_BLOCK
    vec = pl.BlockSpec((1, MOBA_HEADS, HEAD_DIM), lambda b, *_: (b, 0, 0))
    return pl.pallas_call(
        functools.partial(_moba_sample_kernel, n_pages, layer),
        grid_spec=pltpu.PrefetchScalarGridSpec(
            num_scalar_prefetch=2,
            grid=(B,),
            in_specs=[vec, vec, vec, pl.BlockSpec(memory_space=pl.ANY), pl.BlockSpec(memory_space=pl.ANY)],
            out_specs=vec,
            scratch_shapes=[pltpu.VMEM((MOBA_HEADS, HEAD_DIM, keys), jnp.float32),
                            pltpu.VMEM((MOBA_HEADS, HEAD_DIM, keys), jnp.float32),
                            pltpu.SemaphoreType.DMA((2,))]),
        out_shape=jax.ShapeDtypeStruct((B, MOBA_HEADS, HEAD_DIM), jnp.float32),
        compiler_params=pltpu.CompilerParams(
            dimension_semantics=("arbitrary",), vmem_limit_bytes=VMEM_LIMIT_BYTES),
        name="moba_sample",
    )(pt, sel_flat, q, k_new, v_new, ck, cv)


def layer_norm(x, g, b):
    xf = x.astype(jnp.float32)
    mu = jnp.mean(xf, -1, keepdims=True)
    var = jnp.mean(jnp.square(xf - mu), -1, keepdims=True)
    return ((xf - mu) * lax.rsqrt(var + LN_EPS)).astype(x.dtype) * g + b


def rope(x, pos):
    half = x.shape[-1] // 2
    inv = ROPE_THETA ** (-jnp.arange(half, dtype=jnp.float32) / half)
    ang = pos.astype(jnp.float32)[:, None] * inv[None, :]
    cos = jnp.cos(ang)[None, :, None, :]
    sin = jnp.sin(ang)[None, :, None, :]
    x1 = x[..., :half].astype(jnp.float32)
    x2 = x[..., half:].astype(jnp.float32)
    return jnp.concatenate([x1 * cos - x2 * sin, x2 * cos + x1 * sin], -1).astype(x.dtype)


WKV_TIME_CHUNK = 32


def _wkv_scan_kernel(vh, w_ref, nkk_ref, kka_ref, k_ref, r_ref, v_ref, s0_ref, y_ref, sT_ref, s_ref,
                     vec_ref):
    tc = pl.program_id(1)
    steps = w_ref.shape[0]
    nvi = s_ref.shape[0]

    @pl.when(tc == 0)
    def _():
        s_ref[...] = s0_ref[...]

    def step(t, carry):
        for vi in range(nvi):
            s = s_ref[vi]
            sa = jnp.sum(s * vec_ref[1, t], axis=0, keepdims=True)
            s_new = s * vec_ref[0, t] + sa * vec_ref[2, t] + v_ref[t, vi:vi + 1, :] * vec_ref[3, t]
            s_ref[vi] = s_new
            y_ref[t, vi:vi + 1, :] = jnp.sum(s_new * vec_ref[4, t], axis=0, keepdims=True)
        return carry

    def stage(t, carry):
        for slot, ref in enumerate((w_ref, nkk_ref, kka_ref, k_ref, r_ref)):
            x = ref[t]
            vec_ref[slot, t] = x if vh == 1 else jnp.concatenate([x] * vh, axis=-1)
        return carry

    lax.fori_loop(0, steps, stage, 0)
    lax.fori_loop(0, steps, step, 0)

    @pl.when(tc == pl.num_programs(1) - 1)
    def _():
        sT_ref[...] = s_ref[...]


def wkv_scan_pallas(r, decay, k, v, kk, a, s0):
    B, T, H, N = r.shape
    bh = B * H
    vh = max(1, LANES // bh)
    nvi = N // vh
    width = vh * bh
    assert width % LANES == 0
    tcs = min(WKV_TIME_CHUNK, T)
    assert T % tcs == 0

    def key_major(x):
        return jnp.transpose(x, (1, 3, 0, 2)).reshape(T, N, bh)

    v_l = jnp.transpose(v.reshape(B, T, H, vh, nvi), (1, 4, 3, 0, 2)).reshape(T, nvi, width)
    s0_l = jnp.transpose(s0.reshape(B, H, vh, nvi, N), (3, 4, 2, 0, 1)).reshape(nvi, N, width)
    kspec = pl.BlockSpec((tcs, N, LANES // vh), lambda j, t: (t, 0, j))
    vspec = pl.BlockSpec((tcs, nvi, LANES), lambda j, t: (t, 0, j))
    sspec = pl.BlockSpec((nvi, N, LANES), lambda j, t: (0, 0, j))
    y_l, sT_l = pl.pallas_call(
        functools.partial(_wkv_scan_kernel, vh),
        grid=(width // LANES, T // tcs),
        in_specs=[kspec, kspec, kspec, kspec, kspec, vspec, sspec],
        out_specs=[vspec, sspec],
        out_shape=[jax.ShapeDtypeStruct((T, nvi, width), jnp.float32),
                   jax.ShapeDtypeStruct((nvi, N, width), jnp.float32)],
        scratch_shapes=[pltpu.VMEM((nvi, N, LANES), jnp.float32), pltpu.VMEM((5, tcs, N, LANES), jnp.float32)],
        compiler_params=pltpu.CompilerParams(
            dimension_semantics=("parallel", "arbitrary"), vmem_limit_bytes=VMEM_LIMIT_BYTES),
        name="wkv_scan",
    )(key_major(decay), key_major(-kk), key_major(kk * a), key_major(k), key_major(r), v_l, s0_l)
    y = jnp.transpose(y_l.reshape(T, nvi, vh, B, H), (3, 0, 4, 2, 1)).reshape(B, T, H, N)
    sT = jnp.transpose(sT_l.reshape(nvi, N, vh, B, H), (3, 4, 2, 0, 1)).reshape(B, H, N, N)
    return y, sT


def rwkv_group(proj, prev, s0, mu, w0, w2, a0, a2, g2, k_k, k_a, r_k, lnx_g, lnx_b):
    B, T, _ = proj.shape
    shifted = jnp.concatenate([prev[:, None, :].astype(proj.dtype), proj[:, :-1]], axis=1)
    xm = proj + mu * (shifted - proj)
    c1, c2, c3 = RWKV_WIDTH, 2 * RWKV_WIDTH, 3 * RWKV_WIDTH
    c4 = c3 + LORA_W
    c5 = c4 + LORA_A
    r, k, v = xm[..., :c1], xm[..., c1:c2], xm[..., c2:c3]
    xw, xa, xg = xm[..., c3:c4], xm[..., c4:c5], xm[..., c5:]
    w = -jax.nn.softplus(-(w0 + jnp.tanh(xw) @ w2)) - 0.5
    decay = jnp.exp(-jnp.exp(w.astype(jnp.float32)))
    a = jax.nn.sigmoid(a0 + xa @ a2)
    g = jax.nn.sigmoid(xg) @ g2
    heads = lambda t: t.astype(jnp.float32).reshape(B, T, RWKV_HEADS, HEAD_DIM)
    kk = heads(k * k_k)
    kk = kk / jnp.maximum(jnp.sqrt(jnp.sum(kk * kk, -1, keepdims=True)), 1e-12)
    k = k * (1.0 + (a - 1.0) * k_a)
    r_h, k_h, v_h, a_h, w_h = heads(r), heads(k), heads(v), heads(a), heads(decay)
    y, s = wkv_scan_pallas(r_h, w_h, k_h, v_h, kk, a_h, s0.astype(jnp.float32))
    mean = jnp.mean(y, -1, keepdims=True)
    var = jnp.mean(jnp.square(y - mean), -1, keepdims=True)
    y = ((y - mean) * lax.rsqrt(var + GN_EPS)).reshape(B, T, RWKV_WIDTH) * lnx_g + lnx_b
    bonus = (jnp.sum(r_h * k_h * r_k, -1, keepdims=True) * v_h).reshape(B, T, RWKV_WIDTH)
    out = ((y + bonus) * g).astype(proj.dtype)
    return out, s.astype(s0.dtype), proj[:, -1]


def mem_attend(x, mk, mv, wq, wo):
    B, T, _ = x.shape
    q = pmatmul(x.reshape(B * T, D_MODEL), wq).reshape(B, T, MEM_HEADS, MEM_HEAD_DIM)
    s = jnp.einsum('bthd,bmhd->bhtm', q, mk.astype(q.dtype), preferred_element_type=jnp.float32) * MEM_HEAD_DIM ** -0.5
    prob = jax.nn.softmax(s, axis=-1).astype(x.dtype)
    o = jnp.einsum('bhtm,bmhd->bthd', prob, mv.astype(x.dtype)).reshape(B * T, D_MODEL)
    return pmatmul(o, wo).reshape(B, T, D_MODEL)


def run_trunk(x, pos0, shift0, wkv0, cache_k, cache_v, page_table, mem_k, mem_v, p):
    B, T, _ = x.shape
    pos = pos0 + jnp.arange(T, dtype=jnp.int32)
    k_rows, v_rows, wkvs, shifts = [], [], [], []
    for i in range(DEPTH):
        proj = pmatmul(x.reshape(B * T, D_MODEL), p['w_in'][i]).reshape(B, T, IN_PROJ)
        rw_out, wkv_i, shift_i = rwkv_group(
            proj[..., :RWKV_PROJ], shift0[i], wkv0[i], p['shift_mu'][i],
            p['rwkv_w0'][i], p['rwkv_w2'][i], p['rwkv_a0'][i], p['rwkv_a2'][i],
            p['rwkv_g2'][i], p['rwkv_k_k'][i], p['rwkv_k_a'][i], p['rwkv_r_k'][i],
            p['rwkv_lnx_g'][i], p['rwkv_lnx_b'][i])
        q, k, v = jnp.split(proj[..., RWKV_PROJ:], 3, axis=-1)
        q = rope(q.reshape(B, T, MOBA_HEADS, HEAD_DIM), pos)
        k = rope(k.reshape(B, T, MOBA_HEADS, HEAD_DIM), pos)
        v = v.reshape(B, T, MOBA_HEADS, HEAD_DIM)
        if cache_k is None:
            mb_out = moba_prompt(q.reshape(B, T, MOBA_WIDTH), k.reshape(B, T, MOBA_WIDTH),
                                 v.reshape(B, T, MOBA_WIDTH))
        else:
            assert T == 1 and pos0 % MOBA_BLOCK == 0
            mb_out = moba_sample(q[:, 0], k[:, 0], v[:, 0], cache_k, cache_v, i,
                                 page_table).reshape(B, T, MOBA_WIDTH)
        mix = pmatmul(jnp.concatenate([rw_out, mb_out], axis=-1).reshape(B * T, D_MODEL),
                      p['w_out'][i]).reshape(B, T, D_MODEL)
        x = layer_norm(DEEPNORM_ALPHA * x + mix, p['ln1_g'][i], p['ln1_b'][i])
        cross = mem_attend(x, mem_k[i], mem_v[i], p['w_mem_q'][i], p['w_mem_o'][i])
        x = layer_norm(DEEPNORM_ALPHA * x + cross, p['ln2_g'][i], p['ln2_b'][i])
        x = peer_ffn_ln(x, p['peer_w_q_bf16'][i], p['peer_sub_keys_bf16'][i], p['peer_uv'][i],
                        p['ln3_g'][i], p['ln3_b'][i])
        k_rows.append(k)
        v_rows.append(v)
        wkvs.append(wkv_i)
        shifts.append(shift_i)
    return x, jnp.stack(k_rows), jnp.stack(v_rows), jnp.stack(wkvs), jnp.stack(shifts)


def kernel(x_prompt, x_sample, mem_prompt, cache_k, cache_v, page_table, state_wkv,
           state_shift, cache_mem_k, cache_mem_v, w_in, shift_mu, rwkv_w0, rwkv_w2,
           rwkv_a0, rwkv_a2, rwkv_g2, rwkv_k_k, rwkv_k_a, rwkv_r_k, rwkv_lnx_g,
           rwkv_lnx_b, w_out, ln1_g, ln1_b, w_mem_q, w_mem_kv, w_mem_o, ln2_g, ln2_b,
           peer_w_q, peer_sub_keys, peer_u, peer_v, ln3_g, ln3_b):
    p = dict(w_in=w_in, shift_mu=shift_mu, rwkv_w0=rwkv_w0, rwkv_w2=rwkv_w2,
             rwkv_a0=rwkv_a0, rwkv_a2=rwkv_a2, rwkv_g2=rwkv_g2, rwkv_k_k=rwkv_k_k,
             rwkv_k_a=rwkv_k_a, rwkv_r_k=rwkv_r_k, rwkv_lnx_g=rwkv_lnx_g,
             rwkv_lnx_b=rwkv_lnx_b, w_out=w_out, ln1_g=ln1_g, ln1_b=ln1_b,
             w_mem_q=w_mem_q, w_mem_o=w_mem_o, ln2_g=ln2_g, ln2_b=ln2_b,
             ln3_g=ln3_g, ln3_b=ln3_b)
    p['peer_uv'] = jnp.concatenate([peer_u, peer_v], axis=-1)
    p['peer_w_q_bf16'] = peer_w_q.astype(jnp.bfloat16)
    p['peer_sub_keys_bf16'] = peer_sub_keys.astype(jnp.bfloat16)
    B = x_prompt.shape[0]
    mem_len = mem_prompt.shape[1]
    mkv = jnp.stack([pmatmul(mem_prompt.reshape(B * mem_len, D_MODEL), w_mem_kv[i]).reshape(B, mem_len, 2 * D_MODEL)
                     for i in range(DEPTH)])
    mem_k_prompt = mkv[..., :D_MODEL].reshape(DEPTH, B, mem_len, MEM_HEADS, MEM_HEAD_DIM)
    mem_v_prompt = mkv[..., D_MODEL:].reshape(DEPTH, B, mem_len, MEM_HEADS, MEM_HEAD_DIM)
    shift0 = jnp.zeros((DEPTH, B, RWKV_PROJ), x_prompt.dtype)
    wkv0 = jnp.zeros((DEPTH, B, RWKV_HEADS, HEAD_DIM, HEAD_DIM), x_prompt.dtype)
    past_len = page_table.shape[1] * PAGE_SIZE
    y_sample, k_sample, v_sample, wkv_sample, shift_sample = run_trunk(
        x_sample, past_len, state_shift, state_wkv, cache_k, cache_v, page_table,
        cache_mem_k, cache_mem_v, p)
    y_prompt, k_prompt, v_prompt, wkv_prompt, shift_prompt = run_trunk(
        x_prompt, 0, shift0, wkv0, None, None, None, mem_k_prompt, mem_v_prompt, p)
    return (y_prompt, y_sample, k_prompt, v_prompt, wkv_prompt, shift_prompt,
            mem_k_prompt, mem_v_prompt, k_sample, v_sample, wkv_sample, shift_sample)
```

```python
import functools

import jax
import jax.numpy as jnp
from jax import lax
from jax.experimental import pallas as pl
from jax.experimental.pallas import tpu as pltpu

D_MODEL = 1024
DEPTH = 2
PAGE_SIZE = 128
HEAD_DIM = 64
RWKV_WIDTH = D_MODEL // 2
MOBA_WIDTH = D_MODEL - RWKV_WIDTH
RWKV_HEADS = RWKV_WIDTH // HEAD_DIM
MOBA_HEADS = MOBA_WIDTH // HEAD_DIM
LORA_W = max(32, int(round(1.8 * D_MODEL ** 0.5 / 32)) * 32)
LORA_A = max(32, int(round(1.8 * D_MODEL ** 0.5 / 32)) * 32)
LORA_G = max(32, int(round(0.6 * D_MODEL ** 0.8 / 32)) * 32)
RWKV_PROJ = 3 * RWKV_WIDTH + LORA_W + LORA_A + LORA_G
MOBA_PROJ = 3 * MOBA_WIDTH
IN_PROJ = RWKV_PROJ + MOBA_PROJ
GN_EPS = 64e-5
MOBA_BLOCK = 256
MOBA_TOPK = 3
ROPE_THETA = 10000.0
NEG_INF = -1e30
MEM_HEADS = 4
MEM_HEAD_DIM = D_MODEL // MEM_HEADS
PEER_HEADS = 8
PEER_NKEYS = 128
PEER_TOPK = 16
PEER_DKEY = 256
PEER_HALF = PEER_DKEY // 2
DEEPNORM_ALPHA = (2 * DEPTH) ** 0.25
LN_EPS = 1e-5

VMEM_LIMIT_BYTES = 56 * 1024 * 1024
SUBLANES = 8
LANES = 128


def _matmul_kernel(x_ref, w_ref, o_ref):
    o_ref[...] = jnp.dot(x_ref[...].astype(jnp.bfloat16), w_ref[...].astype(jnp.bfloat16),
                         preferred_element_type=jnp.float32)


def pmatmul(x, w, tm=512, tn=512):
    m, k = x.shape
    n = w.shape[1]
    tm = min(tm, m)
    assert m % tm == 0
    if n % tn != 0:
        tn = n
    return pl.pallas_call(
        _matmul_kernel,
        grid=(m // tm, n // tn),
        in_specs=[pl.BlockSpec((tm, k), lambda i, j: (i, 0)),
                  pl.BlockSpec((k, tn), lambda i, j: (0, j))],
        out_specs=pl.BlockSpec((tm, tn), lambda i, j: (i, j)),
        out_shape=jax.ShapeDtypeStruct((m, n), jnp.float32),
        compiler_params=pltpu.CompilerParams(
            dimension_semantics=("parallel", "parallel"),
            vmem_limit_bytes=VMEM_LIMIT_BYTES),
    )(x, w)


PEER_ROUTE_TOKENS = 256
PEER_GATHER_TOKENS = 8
PEER_SLOTS = PEER_HEADS * PEER_TOPK


def _topk_sublane(s, k):
    n_rows = s.shape[0]
    iota = lax.broadcasted_iota(jnp.int32, s.shape, 0)
    vals, idxs = [], []
    for _ in range(k):
        m = jnp.max(s, axis=0, keepdims=True)
        idx = jnp.min(jnp.where(s == m, iota, n_rows), axis=0, keepdims=True)
        vals.append(m)
        idxs.append(idx)
        s = jnp.where(iota == idx, -jnp.inf, s)
    return vals, idxs


def _peer_route_kernel(x_ref, wq_ref, sk_ref, e_ref, g_ref):
    q = jnp.dot(x_ref[...].astype(jnp.bfloat16), wq_ref[...], preferred_element_type=jnp.float32)
    nt = (((1,), (1,)), ((), ()))
    for h in range(PEER_HEADS):
        base = h * PEER_DKEY
        q1 = q[:, base:base + PEER_HALF].astype(jnp.bfloat16)
        q2 = q[:, base + PEER_HALF:base + PEER_DKEY].astype(jnp.bfloat16)
        s1 = lax.dot_general(sk_ref[0], q1, nt, preferred_element_type=jnp.float32)
        s2 = lax.dot_general(sk_ref[1], q2, nt, preferred_element_type=jnp.float32)
        v1_rows, i1_rows = _topk_sublane(s1, PEER_TOPK)
        v2_rows, i2_rows = _topk_sublane(s2, PEER_TOPK)
        v2 = jnp.concatenate(v2_rows, axis=0)
        i2 = jnp.concatenate(i2_rows, axis=0)
        tt = s1.shape[1]
        v2s = [v2] + [jnp.concatenate(v2_rows[:SUBLANES], axis=0)] * (PEER_TOPK - 1)
        i2s = [i2] + [jnp.concatenate(i2_rows[:SUBLANES], axis=0)] * (PEER_TOPK - 1)
        subs = ([lax.broadcasted_iota(jnp.int32, (PEER_TOPK, tt), 0)]
                + [lax.broadcasted_iota(jnp.int32, (SUBLANES, tt), 0)] * (PEER_TOPK - 1))
        cand = jnp.concatenate([v1_rows[a] + v2s[a] for a in range(PEER_TOPK)], axis=0)
        cand_idx = jnp.concatenate([i1_rows[a] * PEER_NKEYS + i2s[a] for a in range(PEER_TOPK)], axis=0)
        pos_all = jnp.concatenate([a * PEER_TOPK + subs[a] for a in range(PEER_TOPK)], axis=0)
        n_cand = PEER_TOPK * PEER_TOPK
        top_s, top_e = [], []
        for _ in range(PEER_TOPK):
            m = jnp.max(cand, axis=0, keepdims=True)
            pos = jnp.min(jnp.where(cand == m, pos_all, n_cand), axis=0, keepdims=True)
            hit = pos_all == pos
            top_s.append(m)
            top_e.append(jnp.sum(jnp.where(hit, cand_idx, 0), axis=0, keepdims=True))
            cand = jnp.where(hit, -jnp.inf, cand)
        top_s = jnp.concatenate(top_s, axis=0)
        p = jnp.exp(top_s - top_s[0:1, :])
        g = p / jnp.sum(p, axis=0, keepdims=True)
        e_ref[h * PEER_TOPK:(h + 1) * PEER_TOPK, :] = jnp.concatenate(top_e, axis=0)
        g_ref[h * PEER_TOPK:(h + 1) * PEER_TOPK, :] = g


def peer_route(x, wq_bf16, sk_bf16):
    n = x.shape[0]
    tt = min(PEER_ROUTE_TOKENS, n)
    assert n % tt == 0
    return pl.pallas_call(
        _peer_route_kernel,
        grid=(n // tt,),
        in_specs=[pl.BlockSpec((tt, D_MODEL), lambda i: (i, 0)),
                  pl.BlockSpec((D_MODEL, PEER_HEADS * PEER_DKEY), lambda i: (0, 0)),
                  pl.BlockSpec((2, PEER_NKEYS, PEER_HALF), lambda i: (0, 0, 0))],
        out_specs=[pl.BlockSpec((PEER_SLOTS, tt), lambda i: (0, i)),
                   pl.BlockSpec((PEER_SLOTS, tt), lambda i: (0, i))],
        out_shape=[jax.ShapeDtypeStruct((PEER_SLOTS, n), jnp.int32),
                   jax.ShapeDtypeStruct((PEER_SLOTS, n), jnp.float32)],
        compiler_params=pltpu.CompilerParams(
            dimension_semantics=("parallel",), vmem_limit_bytes=VMEM_LIMIT_BYTES),
        name="peer_route",
    )(x, wq_bf16, sk_bf16)


def _peer_gather_kernel(e_hbm, x_ref, g_ref, lng_ref, lnb_ref, uv_hbm, o_ref,
                        idx_smem, buf, sem_idx, sem_rows):
    i = pl.program_id(0)
    n = pl.num_programs(0)
    rows = PEER_GATHER_TOKENS * PEER_SLOTS
    groups = PEER_SLOTS // SUBLANES
    half = D_MODEL // LANES

    def idx_copy(tile, slot):
        return pltpu.make_async_copy(e_hbm.at[pl.ds(tile * rows, rows)], idx_smem.at[slot], sem_idx.at[slot])

    def row_copy(row, slot, r_hi, r_lo):
        return pltpu.make_async_copy(uv_hbm.at[row], buf.at[slot, r_hi, :, r_lo, :], sem_rows.at[slot])

    def rows_wait(slot):
        pltpu.make_async_copy(uv_hbm.at[pl.ds(0, rows)], buf.at[slot], sem_rows.at[slot]).wait()

    last = n - 1

    @pl.when(i == 0)
    def _():
        idx_copy(0, 0).start()
        idx_copy(0, 0).wait()

        def body(r, carry):
            row_copy(idx_smem[0, r], 0, r // SUBLANES, r % SUBLANES).start()
            return carry
        lax.fori_loop(0, rows, body, 0)
        idx_copy(jnp.minimum(1, last), 1).start()

    per_phase = PEER_SLOTS // (2 * half)

    def step(cur, nxt):
        idx_copy(0, nxt).wait()
        idx_copy(jnp.minimum(i + 2, last), cur).start()
        rows_wait(cur)
        g_tile = g_ref[0]
        for t in range(PEER_GATHER_TOKENS):
            def prefetch(phase):
                for r in range(t * PEER_SLOTS + phase * per_phase, t * PEER_SLOTS + (phase + 1) * per_phase):
                    row_copy(idx_smem[nxt, r], nxt, r // SUBLANES, r % SUBLANES).start(priority=r % 2)
            x_t = x_ref[t]
            lo = t * groups
            acc = None
            for lg in range(half):
                prefetch(lg)
                term = buf[cur, lo:lo + groups, lg] * x_t[lg:lg + 1, :]
                acc = term if acc is None else acc + term
            hid = jnp.sum(acc, axis=-1, keepdims=True)
            act = 0.5 * hid * (1.0 + lax.erf(hid * (2.0 ** -0.5)))
            wgt = g_tile[:, t:t + 1].reshape(groups, SUBLANES, 1) * act
            out_rows = []
            for lg in range(half):
                prefetch(half + lg)
                part = jnp.sum(buf[cur, lo:lo + groups, half + lg] * wgt, axis=0)
                out_rows.append(jnp.sum(part, axis=0, keepdims=True))
            z = DEEPNORM_ALPHA * x_t + jnp.concatenate(out_rows, axis=0)
            mu = jnp.sum(jnp.sum(z, axis=1, keepdims=True), axis=0, keepdims=True) * (1.0 / D_MODEL)
            zc = z - mu
            var = jnp.sum(jnp.sum(zc * zc, axis=1, keepdims=True), axis=0, keepdims=True) * (1.0 / D_MODEL)
            o_ref[t] = zc * lax.rsqrt(var + LN_EPS) * lng_ref[...] + lnb_ref[...]

        @pl.when(i == last)
        def _():
            rows_wait(nxt)
            idx_copy(0, cur).wait()

    @pl.when(i % 2 == 0)
    def _():
        step(0, 1)

    @pl.when(i % 2 == 1)
    def _():
        step(1, 0)


def peer_gather_ln(x, e, g, uv, ln_g, ln_b):
    n = x.shape[0]
    tb = PEER_GATHER_TOKENS
    assert n % tb == 0 and n // tb >= 2
    rows = tb * PEER_SLOTS
    lane_groups = D_MODEL // LANES
    g3 = g.reshape(n // tb, tb, PEER_SLOTS).transpose(0, 2, 1)
    y = pl.pallas_call(
        _peer_gather_kernel,
        grid=(n // tb,),
        in_specs=[pl.BlockSpec(memory_space=pl.ANY),
                  pl.BlockSpec((tb, lane_groups, LANES), lambda i: (i, 0, 0)),
                  pl.BlockSpec((1, PEER_SLOTS, tb), lambda i: (i, 0, 0)),
                  pl.BlockSpec((lane_groups, LANES), lambda i: (0, 0)),
                  pl.BlockSpec((lane_groups, LANES), lambda i: (0, 0)),
                  pl.BlockSpec(memory_space=pl.ANY)],
        out_specs=pl.BlockSpec((tb, lane_groups, LANES), lambda i: (i, 0, 0)),
        out_shape=jax.ShapeDtypeStruct((n, lane_groups, LANES), jnp.float32),
        scratch_shapes=[pltpu.SMEM((2, rows), jnp.int32),
                        pltpu.VMEM((2, rows // SUBLANES, 2 * lane_groups, SUBLANES, LANES), jnp.float32),
                        pltpu.SemaphoreType.DMA((2,)),
                        pltpu.SemaphoreType.DMA((2,))],
        compiler_params=pltpu.CompilerParams(
            dimension_semantics=("arbitrary",), vmem_limit_bytes=VMEM_LIMIT_BYTES),
        name="peer_gather",
    )(e.reshape(n * PEER_SLOTS), x.reshape(n, lane_groups, LANES), g3,
      ln_g.reshape(lane_groups, LANES), ln_b.reshape(lane_groups, LANES),
      uv.reshape(uv.shape[0], 2 * lane_groups, LANES))
    return y.reshape(n, D_MODEL)


def peer_ffn_ln(x, wq_bf16, sk_bf16, uv, ln_g, ln_b):
    B, T, D = x.shape
    xf = x.reshape(B * T, D)
    e_t, g_t = peer_route(xf, wq_bf16, sk_bf16)
    y = peer_gather_ln(xf, e_t.T, g_t.T, uv, ln_g, ln_b)
    return y.reshape(B, T, D)


def _moba_prompt_kernel(qt_ref, k_ref, vt_ref, o_ref, kmean_ref, sel_ref):
    qi = pl.program_id(2)
    nb = k_ref.shape[2]
    blk = k_ref.shape[3]
    scale = HEAD_DIM ** -0.5

    @pl.when(qi == 0)
    def _():
        for nblk in range(nb):
            kmean_ref[nblk:nblk + 1, :] = jnp.sum(k_ref[0, 0, nblk], axis=0, keepdims=True) * (1.0 / blk)

    qt = qt_ref[0, 0, 0].astype(jnp.bfloat16)
    gate = jnp.dot(kmean_ref[...].astype(jnp.bfloat16), qt, preferred_element_type=jnp.float32)
    blk_id = lax.broadcasted_iota(jnp.int32, gate.shape, 0)
    gate = jnp.where(blk_id < qi, gate, NEG_INF)
    rank = jnp.zeros(gate.shape, jnp.int32)
    for m in range(nb):
        gm = gate[m:m + 1, :]
        beats = (gm > gate) | ((gm == gate) & (blk_id > m))
        rank = rank + beats.astype(jnp.int32)
    sel_ref[...] = ((rank < MOBA_TOPK) & (blk_id < qi)).astype(jnp.float32)

    key_pos = lax.broadcasted_iota(jnp.int32, (blk, blk), 0)
    qry_pos = lax.broadcasted_iota(jnp.int32, (blk, blk), 1)
    causal = key_pos <= qry_pos

    def attend_pair(j, carry):
        m_run, l_run, acc = carry
        ss, keeps = [], []
        for d in range(2):
            n = 2 * j + d
            keep = (sel_ref[pl.ds(n, 1), :] > 0.5) | jnp.logical_and(n == qi, causal)
            s = jnp.dot(k_ref[0, 0, n].astype(jnp.bfloat16), qt, preferred_element_type=jnp.float32) * scale
            ss.append(jnp.where(keep, s, NEG_INF))
            keeps.append(keep)
        m_new = jnp.maximum(m_run, jnp.maximum(jnp.max(ss[0], axis=0, keepdims=True),
                                               jnp.max(ss[1], axis=0, keepdims=True)))
        alpha = jnp.exp(m_run - m_new)
        l_new = alpha * l_run
        acc_new = alpha * acc
        for d in range(2):
            p = jnp.where(keeps[d], jnp.exp(ss[d] - m_new), 0.0)
            l_new = l_new + jnp.sum(p, axis=0, keepdims=True)
            acc_new = acc_new + jnp.dot(vt_ref[0, 0, 2 * j + d].astype(jnp.bfloat16), p.astype(jnp.bfloat16),
                                        preferred_element_type=jnp.float32)
        return m_new, l_new, acc_new

    init = (jnp.full((1, blk), NEG_INF, jnp.float32), jnp.zeros((1, blk), jnp.float32),
            jnp.zeros((HEAD_DIM, blk), jnp.float32))
    m_run, l_run, acc = lax.fori_loop(0, (qi + 2) // 2, attend_pair, init)
    o_ref[0, 0, 0] = acc / l_run


def moba_prompt(q, k, v):
    B, T, _ = q.shape
    nb = T // MOBA_BLOCK
    assert T % MOBA_BLOCK == 0 and nb % 2 == 0
    split = lambda t: t.reshape(B, nb, MOBA_BLOCK, MOBA_HEADS, HEAD_DIM)
    qt = split(q).transpose(0, 3, 1, 4, 2)
    kb = split(k).transpose(0, 3, 1, 2, 4)
    vt = split(v).transpose(0, 3, 1, 4, 2)
    ot = pl.pallas_call(
        _moba_prompt_kernel,
        grid=(B, MOBA_HEADS, nb),
        in_specs=[pl.BlockSpec((1, 1, 1, HEAD_DIM, MOBA_BLOCK), lambda b, h, i: (b, h, i, 0, 0)),
                  pl.BlockSpec((1, 1, nb, MOBA_BLOCK, HEAD_DIM), lambda b, h, i: (b, h, 0, 0, 0)),
                  pl.BlockSpec((1, 1, nb, HEAD_DIM, MOBA_BLOCK), lambda b, h, i: (b, h, 0, 0, 0))],
        out_specs=pl.BlockSpec((1, 1, 1, HEAD_DIM, MOBA_BLOCK), lambda b, h, i: (b, h, i, 0, 0)),
        out_shape=jax.ShapeDtypeStruct((B, MOBA_HEADS, nb, HEAD_DIM, MOBA_BLOCK), jnp.float32),
        scratch_shapes=[pltpu.VMEM((nb, HEAD_DIM), jnp.float32),
                        pltpu.VMEM((nb, MOBA_BLOCK), jnp.float32)],
        compiler_params=pltpu.CompilerParams(
            dimension_semantics=("parallel", "parallel", "arbitrary"), vmem_limit_bytes=VMEM_LIMIT_BYTES),
        name="moba_prompt",
    )(qt, kb, vt)
    return ot.transpose(0, 2, 4, 1, 3).reshape(B, T, MOBA_HEADS * HEAD_DIM)


MOBA_GATE_PAGES = 8


def _moba_gate_kernel(pt_ref, q_ref, *refs):
    del pt_ref
    page_refs, sel_ref, gate_ref = refs[:MOBA_GATE_PAGES], refs[MOBA_GATE_PAGES], refs[MOBA_GATE_PAGES + 1]
    p = pl.program_id(1)
    pages_per_block = MOBA_BLOCK // PAGE_SIZE
    blocks_per_step = MOBA_GATE_PAGES // pages_per_block
    lane = lax.broadcasted_iota(jnp.int32, (MOBA_HEADS, LANES), 1)

    @pl.when(p == 0)
    def _():
        gate_ref[...] = jnp.full(gate_ref.shape, NEG_INF, jnp.float32)

    q = q_ref[0].astype(jnp.bfloat16).astype(jnp.float32)
    gate = gate_ref[...]
    for j in range(blocks_per_step):
        block = page_refs[j * pages_per_block][0, 0]
        for h in range(1, pages_per_block):
            block = block + page_refs[j * pages_per_block + h][0, 0]
        k_mean = (jnp.sum(block, axis=-1, keepdims=True) * (1.0 / MOBA_BLOCK)).astype(jnp.bfloat16)
        g = jnp.sum(k_mean.astype(jnp.float32) * q, axis=1)
        gate = jnp.where(lane == p * blocks_per_step + j, g, gate)
    gate_ref[...] = gate

    @pl.when(p == pl.num_programs(1) - 1)
    def _():
        rank = jnp.zeros(gate.shape, jnp.int32)
        for m in range(LANES):
            gm = gate[:, m:m + 1]
            beats = (gm > gate) | ((gm == gate) & (lane > m))
            rank = rank + beats.astype(jnp.int32)
        sel = jnp.zeros(gate.shape, jnp.int32)
        for r in range(MOBA_TOPK):
            idx = jnp.sum(jnp.where(rank == r, lane, 0), axis=1, keepdims=True)
            sel = jnp.where(lane == r, idx, sel)
        sel_ref[0] = sel


def _moba_sample_kernel(n_pages, layer, pt_ref, sel_ref, q_ref, kn_ref, vn_ref, ck_hbm, cv_hbm, o_ref,
                        kbuf, vbuf, sem):
    b = pl.program_id(0)
    pages_per_block = MOBA_BLOCK // PAGE_SIZE
    copies = []
    for h in range(MOBA_HEADS):
        for r in range(MOBA_TOPK):
            blk = sel_ref[(b * MOBA_HEADS + h) * MOBA_TOPK + r]
            for half in range(pages_per_block):
                page = pt_ref[b * n_pages + blk * pages_per_block + half]
                dst = pl.ds((r * pages_per_block + half) * PAGE_SIZE, PAGE_SIZE)
                copies.append(pltpu.make_async_copy(ck_hbm.at[layer, page, h], kbuf.at[h, :, dst], sem.at[0]))
                copies.append(pltpu.make_async_copy(cv_hbm.at[layer, page, h], vbuf.at[h, :, dst], sem.at[1]))
    for c in copies:
        c.start()
    for c in copies:
        c.wait()
    scale = HEAD_DIM ** -0.5
    nt = (((1,), (1,)), ((), ()))
    for h in range(MOBA_HEADS):
        q = q_ref[0, h:h + 1, :].astype(jnp.bfloat16)
        kn = kn_ref[0, h:h + 1, :].astype(jnp.bfloat16)
        vn = vn_ref[0, h:h + 1, :].astype(jnp.bfloat16)
        s_sel = jnp.dot(q, kbuf[h].astype(jnp.bfloat16), preferred_element_type=jnp.float32) * scale
        s_own = jnp.sum(q.astype(jnp.float32) * kn.astype(jnp.float32), axis=1, keepdims=True) * scale
        m = jnp.maximum(jnp.max(s_sel, axis=1, keepdims=True), s_own)
        p_sel = jnp.exp(s_sel - m)
        p_own = jnp.exp(s_own - m)
        denom = jnp.sum(p_sel, axis=1, keepdims=True) + p_own
        p_sel = (p_sel / denom).astype(jnp.bfloat16)
        p_own = (p_own / denom).astype(jnp.bfloat16).astype(jnp.float32)
        o = lax.dot_general(p_sel, vbuf[h].astype(jnp.bfloat16), nt, preferred_element_type=jnp.float32)
        o_ref[0, h:h + 1, :] = o + p_own * vn.astype(jnp.float32)


def moba_sample(q, k_new, v_new, cache_k, cache_v, layer, page_table):
    B, n_pages = page_table.shape
    assert n_pages % MOBA_GATE_PAGES == 0 and MOBA_GATE_PAGES % (MOBA_BLOCK // PAGE_SIZE) == 0
    assert MOBA_TOPK <= n_pages * PAGE_SIZE // MOBA_BLOCK <= LANES
    pt = page_table.reshape(B * n_pages)
    ck = cache_k.transpose(0, 1, 3, 4, 2)
    cv = cache_v.transpose(0, 1, 3, 4, 2)
    steps = n_pages // MOBA_GATE_PAGES

    def page_spec(j):
        return pl.BlockSpec((1, 1, MOBA_HEADS, HEAD_DIM, PAGE_SIZE),
                            lambda b, p, pt_ref: (layer, pt_ref[b * n_pages + p * MOBA_GATE_PAGES + j], 0, 0, 0))

    sel = pl.pallas_call(
        _moba_gate_kernel,
        grid_spec=pltpu.PrefetchScalarGridSpec(
            num_scalar_prefetch=1,
            grid=(B, steps),
            in_specs=[pl.BlockSpec((1, MOBA_HEADS, HEAD_DIM, 1), lambda b, p, pt_ref: (b, 0, 0, 0))]
                     + [page_spec(j) for j in range(MOBA_GATE_PAGES)],
            out_specs=pl.BlockSpec((1, MOBA_HEADS, LANES), lambda b, p, pt_ref: (b, 0, 0)),
            scratch_shapes=[pltpu.VMEM((MOBA_HEADS, LANES), jnp.float32)]),
        out_shape=jax.ShapeDtypeStruct((B, MOBA_HEADS, LANES), jnp.int32),
        compiler_params=pltpu.CompilerParams(
            dimension_semantics=("parallel", "arbitrary"), vmem_limit_bytes=VMEM_LIMIT_BYTES),
        name="moba_gate",
    )(pt, q.reshape(B, MOBA_HEADS, HEAD_DIM, 1), *([ck] * MOBA_GATE_PAGES))
    sel_flat = sel[:, :, :MOBA_TOPK].reshape(B * MOBA_HEADS * MOBA_TOPK)
    keys = MOBA_TOPK * MOBA_BLOCK
    vec = pl.BlockSpec((1, MOBA_HEADS, HEAD_DIM), lambda b, *_: (b, 0, 0))
    return pl.pallas_call(
        functools.partial(_moba_sample_kernel, n_pages, layer),
        grid_spec=pltpu.PrefetchScalarGridSpec(
            num_scalar_prefetch=2,
            grid=(B,),
            in_specs=[vec, vec, vec, pl.BlockSpec(memory_space=pl.ANY), pl.BlockSpec(memory_space=pl.ANY)],
            out_specs=vec,
            scratch_shapes=[pltpu.VMEM((MOBA_HEADS, HEAD_DIM, keys), jnp.float32),
                            pltpu.VMEM((MOBA_HEADS, HEAD_DIM, keys), jnp.float32),
                            pltpu.SemaphoreType.DMA((2,))]),
        out_shape=jax.ShapeDtypeStruct((B, MOBA_HEADS, HEAD_DIM), jnp.float32),
        compiler_params=pltpu.CompilerParams(
            dimension_semantics=("arbitrary",), vmem_limit_bytes=VMEM_LIMIT_BYTES),
        name="moba_sample",
    )(pt, sel_flat, q, k_new, v_new, ck, cv)


def _layer_norm_rows(z, g, b):
    mu = jnp.mean(z, axis=-1, keepdims=True)
    zc = z - mu
    var = jnp.mean(zc * zc, axis=-1, keepdims=True)
    return zc * lax.rsqrt(var + LN_EPS) * g + b


def _proj_res_ln_kernel(a_ref, w_ref, x_ref, g_ref, b_ref, o_ref):
    y = jnp.dot(a_ref[...].astype(jnp.bfloat16), w_ref[...], preferred_element_type=jnp.float32)
    o_ref[...] = _layer_norm_rows(DEEPNORM_ALPHA * x_ref[...] + y, g_ref[...], b_ref[...])


def proj_res_ln(a, w_bf16, x, ln_g, ln_b, tm=512):
    m = a.shape[0]
    tm = min(tm, m)
    assert m % tm == 0
    row = pl.BlockSpec((tm, D_MODEL), lambda i: (i, 0))
    vec = pl.BlockSpec((1, D_MODEL), lambda i: (0, 0))
    return pl.pallas_call(
        _proj_res_ln_kernel,
        grid=(m // tm,),
        in_specs=[row, pl.BlockSpec((D_MODEL, D_MODEL), lambda i: (0, 0)), row, vec, vec],
        out_specs=row,
        out_shape=jax.ShapeDtypeStruct((m, D_MODEL), jnp.float32),
        compiler_params=pltpu.CompilerParams(
            dimension_semantics=("parallel",), vmem_limit_bytes=VMEM_LIMIT_BYTES),
        name="proj_res_ln",
    )(a, w_bf16, x, ln_g.reshape(1, D_MODEL), ln_b.reshape(1, D_MODEL))


def _mem_attn_kernel(x_ref, wq_ref, wo_ref, mk_ref, mv_ref, g_ref, b_ref, o_ref):
    x = x_ref[0]
    q = jnp.dot(x.astype(jnp.bfloat16), wq_ref[...], preferred_element_type=jnp.float32)
    nt = (((1,), (1,)), ((), ()))
    heads = []
    for h in range(MEM_HEADS):
        sl = slice(h * MEM_HEAD_DIM, (h + 1) * MEM_HEAD_DIM)
        s = lax.dot_general(q[:, sl].astype(jnp.bfloat16), mk_ref[0, :, sl].astype(jnp.bfloat16), nt,
                            preferred_element_type=jnp.float32) * MEM_HEAD_DIM ** -0.5
        p = jnp.exp(s - jnp.max(s, axis=-1, keepdims=True))
        p = p / jnp.sum(p, axis=-1, keepdims=True)
        heads.append(jnp.dot(p.astype(jnp.bfloat16), mv_ref[0, :, sl].astype(jnp.bfloat16),
                             preferred_element_type=jnp.float32))
    o = jnp.concatenate(heads, axis=1)
    cross = jnp.dot(o.astype(jnp.bfloat16), wo_ref[...], preferred_element_type=jnp.float32)
    o_ref[0] = _layer_norm_rows(DEEPNORM_ALPHA * x + cross, g_ref[...], b_ref[...])


def mem_attend_ln(x, mk, mv, wq_bf16, wo_bf16, ln_g, ln_b, tq=512):
    B, T, _ = x.shape
    mem_len = mk.shape[1]
    tq = min(tq, T)
    assert T % tq == 0
    xspec = pl.BlockSpec((1, tq, D_MODEL), lambda b, t: (b, t, 0))
    wspec = pl.BlockSpec((D_MODEL, D_MODEL), lambda b, t: (0, 0))
    mspec = pl.BlockSpec((1, mem_len, D_MODEL), lambda b, t: (b, 0, 0))
    vec = pl.BlockSpec((1, D_MODEL), lambda b, t: (0, 0))
    return pl.pallas_call(
        _mem_attn_kernel,
        grid=(B, T // tq),
        in_specs=[xspec, wspec, wspec, mspec, mspec, vec, vec],
        out_specs=xspec,
        out_shape=jax.ShapeDtypeStruct((B, T, D_MODEL), jnp.float32),
        compiler_params=pltpu.CompilerParams(
            dimension_semantics=("parallel", "parallel"), vmem_limit_bytes=VMEM_LIMIT_BYTES),
        name="mem_attn",
    )(x, wq_bf16, wo_bf16, mk.reshape(B, mem_len, D_MODEL), mv.reshape(B, mem_len, D_MODEL),
      ln_g.reshape(1, D_MODEL), ln_b.reshape(1, D_MODEL))


def rope(x, pos):
    half = x.shape[-1] // 2
    inv = ROPE_THETA ** (-jnp.arange(half, dtype=jnp.float32) / half)
    ang = pos.astype(jnp.float32)[:, None] * inv[None, :]
    cos = jnp.cos(ang)[None, :, None, :]
    sin = jnp.sin(ang)[None, :, None, :]
    x1 = x[..., :half].astype(jnp.float32)
    x2 = x[..., half:].astype(jnp.float32)
    return jnp.concatenate([x1 * cos - x2 * sin, x2 * cos + x1 * sin], -1).astype(x.dtype)


WKV_TIME_CHUNK = 32


def _wkv_scan_kernel(w_ref, nkk_ref, kka_ref, k_ref, r_ref, v_ref, s0_ref, y_ref, sT_ref, s_ref):
    tc = pl.program_id(1)
    steps = w_ref.shape[0]
    nvi = s_ref.shape[0]

    @pl.when(tc == 0)
    def _():
        s_ref[...] = s0_ref[...]

    def step(t, carry):
        w = w_ref[t]
        nkk = nkk_ref[t]
        kka = kka_ref[t]
        kv = k_ref[t]
        r = r_ref[t]
        for vi in range(nvi):
            s = s_ref[vi]
            sa = jnp.sum(s * nkk, axis=0, keepdims=True)
            s_new = s * w + sa * kka + v_ref[t, vi:vi + 1, :] * kv
            s_ref[vi] = s_new
            y_ref[t, vi:vi + 1, :] = jnp.sum(s_new * r, axis=0, keepdims=True)
        return carry

    lax.fori_loop(0, steps, step, 0)

    @pl.when(tc == pl.num_programs(1) - 1)
    def _():
        sT_ref[...] = s_ref[...]


def wkv_scan_pallas(r, decay, k, v, kk, a, s0):
    B, T, H, N = r.shape
    bh = B * H
    vh = max(1, LANES // bh)
    nvi = N // vh
    width = vh * bh
    assert width % LANES == 0
    tcs = min(WKV_TIME_CHUNK, T)
    assert T % tcs == 0

    def key_major(x):
        xt = jnp.transpose(x, (1, 3, 0, 2)).reshape(T, N, bh)
        return jnp.concatenate([xt] * vh, axis=-1)

    v_l = jnp.transpose(v.reshape(B, T, H, vh, nvi), (1, 4, 3, 0, 2)).reshape(T, nvi, width)
    s0_l = jnp.transpose(s0.reshape(B, H, vh, nvi, N), (3, 4, 2, 0, 1)).reshape(nvi, N, width)
    kspec = pl.BlockSpec((tcs, N, LANES), lambda j, t: (t, 0, j))
    vspec = pl.BlockSpec((tcs, nvi, LANES), lambda j, t: (t, 0, j))
    sspec = pl.BlockSpec((nvi, N, LANES), lambda j, t: (0, 0, j))
    y_l, sT_l = pl.pallas_call(
        _wkv_scan_kernel,
        grid=(width // LANES, T // tcs),
        in_specs=[kspec, kspec, kspec, kspec, kspec, vspec, sspec],
        out_specs=[vspec, sspec],
        out_shape=[jax.ShapeDtypeStruct((T, nvi, width), jnp.float32),
                   jax.ShapeDtypeStruct((nvi, N, width), jnp.float32)],
        scratch_shapes=[pltpu.VMEM((nvi, N, LANES), jnp.float32)],
        compiler_params=pltpu.CompilerParams(
            dimension_semantics=("parallel", "arbitrary"), vmem_limit_bytes=VMEM_LIMIT_BYTES),
        name="wkv_scan",
    )(key_major(decay), key_major(-kk), key_major(kk * a), key_major(k), key_major(r), v_l, s0_l)
    y = jnp.transpose(y_l.reshape(T, nvi, vh, B, H), (3, 0, 4, 2, 1)).reshape(B, T, H, N)
    sT = jnp.transpose(sT_l.reshape(nvi, N, vh, B, H), (3, 4, 2, 0, 1)).reshape(B, H, N, N)
    return y, sT


def rwkv_group(proj, prev, s0, mu, w0, w2, a0, a2, g2, k_k, k_a, r_k, lnx_g, lnx_b):
    B, T, _ = proj.shape
    shifted = jnp.concatenate([prev[:, None, :].astype(proj.dtype), proj[:, :-1]], axis=1)
    xm = proj + mu * (shifted - proj)
    c1, c2, c3 = RWKV_WIDTH, 2 * RWKV_WIDTH, 3 * RWKV_WIDTH
    c4 = c3 + LORA_W
    c5 = c4 + LORA_A
    r, k, v = xm[..., :c1], xm[..., c1:c2], xm[..., c2:c3]
    xw, xa, xg = xm[..., c3:c4], xm[..., c4:c5], xm[..., c5:]
    w = -jax.nn.softplus(-(w0 + jnp.tanh(xw) @ w2)) - 0.5
    decay = jnp.exp(-jnp.exp(w.astype(jnp.float32)))
    a = jax.nn.sigmoid(a0 + xa @ a2)
    g = jax.nn.sigmoid(xg) @ g2
    heads = lambda t: t.astype(jnp.float32).reshape(B, T, RWKV_HEADS, HEAD_DIM)
    kk = heads(k * k_k)
    kk = kk / jnp.maximum(jnp.sqrt(jnp.sum(kk * kk, -1, keepdims=True)), 1e-12)
    k = k * (1.0 + (a - 1.0) * k_a)
    r_h, k_h, v_h, a_h, w_h = heads(r), heads(k), heads(v), heads(a), heads(decay)
    y, s = wkv_scan_pallas(r_h, w_h, k_h, v_h, kk, a_h, s0.astype(jnp.float32))
    mean = jnp.mean(y, -1, keepdims=True)
    var = jnp.mean(jnp.square(y - mean), -1, keepdims=True)
    y = ((y - mean) * lax.rsqrt(var + GN_EPS)).reshape(B, T, RWKV_WIDTH) * lnx_g + lnx_b
    bonus = (jnp.sum(r_h * k_h * r_k, -1, keepdims=True) * v_h).reshape(B, T, RWKV_WIDTH)
    out = ((y + bonus) * g).astype(proj.dtype)
    return out, s.astype(s0.dtype), proj[:, -1]


def run_trunk(x, pos0, shift0, wkv0, cache_k, cache_v, page_table, mem_k, mem_v, p):
    B, T, _ = x.shape
    pos = pos0 + jnp.arange(T, dtype=jnp.int32)
    k_rows, v_rows, wkvs, shifts = [], [], [], []
    for i in range(DEPTH):
        proj = pmatmul(x.reshape(B * T, D_MODEL), p['w_in'][i]).reshape(B, T, IN_PROJ)
        rw_out, wkv_i, shift_i = rwkv_group(
            proj[..., :RWKV_PROJ], shift0[i], wkv0[i], p['shift_mu'][i],
            p['rwkv_w0'][i], p['rwkv_w2'][i], p['rwkv_a0'][i], p['rwkv_a2'][i],
            p['rwkv_g2'][i], p['rwkv_k_k'][i], p['rwkv_k_a'][i], p['rwkv_r_k'][i],
            p['rwkv_lnx_g'][i], p['rwkv_lnx_b'][i])
        q, k, v = jnp.split(proj[..., RWKV_PROJ:], 3, axis=-1)
        q = rope(q.reshape(B, T, MOBA_HEADS, HEAD_DIM), pos)
        k = rope(k.reshape(B, T, MOBA_HEADS, HEAD_DIM), pos)
        v = v.reshape(B, T, MOBA_HEADS, HEAD_DIM)
        if cache_k is None:
            mb_out = moba_prompt(q.reshape(B, T, MOBA_WIDTH), k.reshape(B, T, MOBA_WIDTH),
                                 v.reshape(B, T, MOBA_WIDTH))
        else:
            assert T == 1 and pos0 % MOBA_BLOCK == 0
            mb_out = moba_sample(q[:, 0], k[:, 0], v[:, 0], cache_k, cache_v, i,
                                 page_table).reshape(B, T, MOBA_WIDTH)
        x = proj_res_ln(jnp.concatenate([rw_out, mb_out], axis=-1).reshape(B * T, D_MODEL), p['w_out_bf16'][i],
                        x.reshape(B * T, D_MODEL), p['ln1_g'][i], p['ln1_b'][i]).reshape(B, T, D_MODEL)
        x = mem_attend_ln(x, mem_k[i], mem_v[i], p['w_mem_q_bf16'][i], p['w_mem_o_bf16'][i],
                          p['ln2_g'][i], p['ln2_b'][i])
        x = peer_ffn_ln(x, p['peer_w_q_bf16'][i], p['peer_sub_keys_bf16'][i], p['peer_uv'][i],
                        p['ln3_g'][i], p['ln3_b'][i])
        k_rows.append(k)
        v_rows.append(v)
        wkvs.append(wkv_i)
        shifts.append(shift_i)
    return x, jnp.stack(k_rows), jnp.stack(v_rows), jnp.stack(wkvs), jnp.stack(shifts)


def kernel(x_prompt, x_sample, mem_prompt, cache_k, cache_v, page_table, state_wkv,
           state_shift, cache_mem_k, cache_mem_v, w_in, shift_mu, rwkv_w0, rwkv_w2,
           rwkv_a0, rwkv_a2, rwkv_g2, rwkv_k_k, rwkv_k_a, rwkv_r_k, rwkv_lnx_g,
           rwkv_lnx_b, w_out, ln1_g, ln1_b, w_mem_q, w_mem_kv, w_mem_o, ln2_g, ln2_b,
           peer_w_q, peer_sub_keys, peer_u, peer_v, ln3_g, ln3_b):
    p = dict(w_in=w_in, shift_mu=shift_mu, rwkv_w0=rwkv_w0, rwkv_w2=rwkv_w2,
             rwkv_a0=rwkv_a0, rwkv_a2=rwkv_a2, rwkv_g2=rwkv_g2, rwkv_k_k=rwkv_k_k,
             rwkv_k_a=rwkv_k_a, rwkv_r_k=rwkv_r_k, rwkv_lnx_g=rwkv_lnx_g,
             rwkv_lnx_b=rwkv_lnx_b, w_out=w_out, ln1_g=ln1_g, ln1_b=ln1_b,
             w_mem_q=w_mem_q, w_mem_o=w_mem_o, ln2_g=ln2_g, ln2_b=ln2_b,
             ln3_g=ln3_g, ln3_b=ln3_b)
    p['peer_uv'] = jnp.concatenate([peer_u, peer_v], axis=-1)
    p['peer_w_q_bf16'] = peer_w_q.astype(jnp.bfloat16)
    p['peer_sub_keys_bf16'] = peer_sub_keys.astype(jnp.bfloat16)
    p['w_out_bf16'] = w_out.astype(jnp.bfloat16)
    p['w_mem_q_bf16'] = w_mem_q.astype(jnp.bfloat16)
    p['w_mem_o_bf16'] = w_mem_o.astype(jnp.bfloat16)
    B = x_prompt.shape[0]
    mem_len = mem_prompt.shape[1]
    mkv = jnp.stack([pmatmul(mem_prompt.reshape(B * mem_len, D_MODEL), w_mem_kv[i]).reshape(B, mem_len, 2 * D_MODEL)
                     for i in range(DEPTH)])
    mem_k_prompt = mkv[..., :D_MODEL].reshape(DEPTH, B, mem_len, MEM_HEADS, MEM_HEAD_DIM)
    mem_v_prompt = mkv[..., D_MODEL:].reshape(DEPTH, B, mem_len, MEM_HEADS, MEM_HEAD_DIM)
    shift0 = jnp.zeros((DEPTH, B, RWKV_PROJ), x_prompt.dtype)
    wkv0 = jnp.zeros((DEPTH, B, RWKV_HEADS, HEAD_DIM, HEAD_DIM), x_prompt.dtype)
    past_len = page_table.shape[1] * PAGE_SIZE
    y_sample, k_sample, v_sample, wkv_sample, shift_sample = run_trunk(
        x_sample, past_len, state_shift, state_wkv, cache_k, cache_v, page_table,
        cache_mem_k, cache_mem_v, p)
    y_prompt, k_prompt, v_prompt, wkv_prompt, shift_prompt = run_trunk(
        x_prompt, 0, shift0, wkv0, None, None, None, mem_k_prompt, mem_v_prompt, p)
    return (y_prompt, y_sample, k_prompt, v_prompt, wkv_prompt, shift_prompt,
            mem_k_prompt, mem_v_prompt, k_sample, v_sample, wkv_sample, shift_sample)
```

```python
import functools

import jax
import jax.numpy as jnp
from jax import lax
from jax.experimental import pallas as pl
from jax.experimental.pallas import tpu as pltpu

D_MODEL = 1024
DEPTH = 2
PAGE_SIZE = 128
HEAD_DIM = 64
RWKV_WIDTH = D_MODEL // 2
MOBA_WIDTH = D_MODEL - RWKV_WIDTH
RWKV_HEADS = RWKV_WIDTH // HEAD_DIM
MOBA_HEADS = MOBA_WIDTH // HEAD_DIM
LORA_W = max(32, int(round(1.8 * D_MODEL ** 0.5 / 32)) * 32)
LORA_A = max(32, int(round(1.8 * D_MODEL ** 0.5 / 32)) * 32)
LORA_G = max(32, int(round(0.6 * D_MODEL ** 0.8 / 32)) * 32)
RWKV_PROJ = 3 * RWKV_WIDTH + LORA_W + LORA_A + LORA_G
MOBA_PROJ = 3 * MOBA_WIDTH
IN_PROJ = RWKV_PROJ + MOBA_PROJ
GN_EPS = 64e-5
MOBA_BLOCK = 256
MOBA_TOPK = 3
ROPE_THETA = 10000.0
NEG_INF = -1e30
MEM_HEADS = 4
MEM_HEAD_DIM = D_MODEL // MEM_HEADS
PEER_HEADS = 8
PEER_NKEYS = 128
PEER_TOPK = 16
PEER_DKEY = 256
PEER_HALF = PEER_DKEY // 2
DEEPNORM_ALPHA = (2 * DEPTH) ** 0.25
LN_EPS = 1e-5

VMEM_LIMIT_BYTES = 56 * 1024 * 1024
SUBLANES = 8
LANES = 128


def _matmul_kernel(x_ref, w_ref, o_ref):
    o_ref[...] = jnp.dot(x_ref[...].astype(jnp.bfloat16), w_ref[...].astype(jnp.bfloat16),
                         preferred_element_type=jnp.float32)


def pmatmul(x, w, tm=512, tn=512):
    m, k = x.shape
    n = w.shape[1]
    tm = min(tm, m)
    assert m % tm == 0
    if n % tn != 0:
        tn = n
    return pl.pallas_call(
        _matmul_kernel,
        grid=(m // tm, n // tn),
        in_specs=[pl.BlockSpec((tm, k), lambda i, j: (i, 0)),
                  pl.BlockSpec((k, tn), lambda i, j: (0, j))],
        out_specs=pl.BlockSpec((tm, tn), lambda i, j: (i, j)),
        out_shape=jax.ShapeDtypeStruct((m, n), jnp.float32),
        compiler_params=pltpu.CompilerParams(
            dimension_semantics=("parallel", "parallel"),
            vmem_limit_bytes=VMEM_LIMIT_BYTES),
    )(x, w)


def _in_proj_kernel(x_ref, wr_ref, wq_ref, wk_ref, wv_ref, r_ref, q_ref, k_ref, v_ref):
    x = x_ref[...].astype(jnp.bfloat16)
    for w_ref, o_ref in ((wr_ref, r_ref), (wq_ref, q_ref), (wk_ref, k_ref), (wv_ref, v_ref)):
        o_ref[...] = jnp.dot(x, w_ref[...], preferred_element_type=jnp.float32)


def in_proj(x, w_in_bf16, tm=512):
    m = x.shape[0]
    tm = min(tm, m)
    assert m % tm == 0
    bounds = (0, RWKV_PROJ, RWKV_PROJ + MOBA_WIDTH, RWKV_PROJ + 2 * MOBA_WIDTH, IN_PROJ)
    ws = [w_in_bf16[:, lo:hi] for lo, hi in zip(bounds[:-1], bounds[1:])]
    widths = [w.shape[1] for w in ws]
    return pl.pallas_call(
        _in_proj_kernel,
        grid=(m // tm,),
        in_specs=[pl.BlockSpec((tm, D_MODEL), lambda i: (i, 0))]
                 + [pl.BlockSpec((D_MODEL, n), lambda i: (0, 0)) for n in widths],
        out_specs=[pl.BlockSpec((tm, n), lambda i: (i, 0)) for n in widths],
        out_shape=[jax.ShapeDtypeStruct((m, n), jnp.float32) for n in widths],
        compiler_params=pltpu.CompilerParams(
            dimension_semantics=("parallel",), vmem_limit_bytes=VMEM_LIMIT_BYTES),
        name="in_proj",
    )(x, *ws)


PEER_ROUTE_TOKENS = 256
PEER_GATHER_TOKENS = 8
PEER_SLOTS = PEER_HEADS * PEER_TOPK


def _topk_sublane(s, k):
    n_rows = s.shape[0]
    iota = lax.broadcasted_iota(jnp.int32, s.shape, 0)
    vals, idxs = [], []
    for _ in range(k):
        m = jnp.max(s, axis=0, keepdims=True)
        idx = jnp.min(jnp.where(s == m, iota, n_rows), axis=0, keepdims=True)
        vals.append(m)
        idxs.append(idx)
        s = jnp.where(iota == idx, -jnp.inf, s)
    return vals, idxs


def _peer_route_kernel(x_ref, wq_ref, sk_ref, e_ref, g_ref):
    q = jnp.dot(x_ref[...].astype(jnp.bfloat16), wq_ref[...], preferred_element_type=jnp.float32)
    nt = (((1,), (1,)), ((), ()))
    for h in range(PEER_HEADS):
        base = h * PEER_DKEY
        q1 = q[:, base:base + PEER_HALF].astype(jnp.bfloat16)
        q2 = q[:, base + PEER_HALF:base + PEER_DKEY].astype(jnp.bfloat16)
        s1 = lax.dot_general(sk_ref[0], q1, nt, preferred_element_type=jnp.float32)
        s2 = lax.dot_general(sk_ref[1], q2, nt, preferred_element_type=jnp.float32)
        v1_rows, i1_rows = _topk_sublane(s1, PEER_TOPK)
        v2_rows, i2_rows = _topk_sublane(s2, PEER_TOPK)
        v2 = jnp.concatenate(v2_rows, axis=0)
        i2 = jnp.concatenate(i2_rows, axis=0)
        tt = s1.shape[1]
        v2s = [v2] + [jnp.concatenate(v2_rows[:SUBLANES], axis=0)] * (PEER_TOPK - 1)
        i2s = [i2] + [jnp.concatenate(i2_rows[:SUBLANES], axis=0)] * (PEER_TOPK - 1)
        subs = ([lax.broadcasted_iota(jnp.int32, (PEER_TOPK, tt), 0)]
                + [lax.broadcasted_iota(jnp.int32, (SUBLANES, tt), 0)] * (PEER_TOPK - 1))
        cand = jnp.concatenate([v1_rows[a] + v2s[a] for a in range(PEER_TOPK)], axis=0)
        cand_idx = jnp.concatenate([i1_rows[a] * PEER_NKEYS + i2s[a] for a in range(PEER_TOPK)], axis=0)
        pos_all = jnp.concatenate([a * PEER_TOPK + subs[a] for a in range(PEER_TOPK)], axis=0)
        n_cand = PEER_TOPK * PEER_TOPK
        top_s, top_e = [], []
        for _ in range(PEER_TOPK):
            m = jnp.max(cand, axis=0, keepdims=True)
            pos = jnp.min(jnp.where(cand == m, pos_all, n_cand), axis=0, keepdims=True)
            hit = pos_all == pos
            top_s.append(m)
            top_e.append(jnp.sum(jnp.where(hit, cand_idx, 0), axis=0, keepdims=True))
            cand = jnp.where(hit, -jnp.inf, cand)
        top_s = jnp.concatenate(top_s, axis=0)
        p = jnp.exp(top_s - top_s[0:1, :])
        g = p / jnp.sum(p, axis=0, keepdims=True)
        e_ref[h * PEER_TOPK:(h + 1) * PEER_TOPK, :] = jnp.concatenate(top_e, axis=0)
        g_ref[h * PEER_TOPK:(h + 1) * PEER_TOPK, :] = g


def peer_route(x, wq_bf16, sk_bf16):
    n = x.shape[0]
    tt = min(PEER_ROUTE_TOKENS, n)
    assert n % tt == 0
    return pl.pallas_call(
        _peer_route_kernel,
        grid=(n // tt,),
        in_specs=[pl.BlockSpec((tt, D_MODEL), lambda i: (i, 0)),
                  pl.BlockSpec((D_MODEL, PEER_HEADS * PEER_DKEY), lambda i: (0, 0)),
                  pl.BlockSpec((2, PEER_NKEYS, PEER_HALF), lambda i: (0, 0, 0))],
        out_specs=[pl.BlockSpec((PEER_SLOTS, tt), lambda i: (0, i)),
                   pl.BlockSpec((PEER_SLOTS, tt), lambda i: (0, i))],
        out_shape=[jax.ShapeDtypeStruct((PEER_SLOTS, n), jnp.int32),
                   jax.ShapeDtypeStruct((PEER_SLOTS, n), jnp.float32)],
        compiler_params=pltpu.CompilerParams(
            dimension_semantics=("parallel",), vmem_limit_bytes=VMEM_LIMIT_BYTES),
        name="peer_route",
    )(x, wq_bf16, sk_bf16)


def _peer_gather_kernel(e_hbm, x_ref, g_ref, lng_ref, lnb_ref, uv_hbm, o_ref,
                        idx_smem, buf, sem_idx, sem_rows):
    i = pl.program_id(0)
    n = pl.num_programs(0)
    rows = PEER_GATHER_TOKENS * PEER_SLOTS
    groups = PEER_SLOTS // SUBLANES
    half = D_MODEL // LANES

    def idx_copy(tile, slot):
        return pltpu.make_async_copy(e_hbm.at[pl.ds(tile * rows, rows)], idx_smem.at[slot], sem_idx.at[slot])

    def row_copy(row, slot, r_hi, r_lo):
        return pltpu.make_async_copy(uv_hbm.at[row], buf.at[slot, r_hi, :, r_lo, :], sem_rows.at[slot])

    def rows_wait(slot):
        pltpu.make_async_copy(uv_hbm.at[pl.ds(0, rows)], buf.at[slot], sem_rows.at[slot]).wait()

    last = n - 1

    @pl.when(i == 0)
    def _():
        idx_copy(0, 0).start()
        idx_copy(0, 0).wait()

        def body(r, carry):
            row_copy(idx_smem[0, r], 0, r // SUBLANES, r % SUBLANES).start()
            return carry
        lax.fori_loop(0, rows, body, 0)
        idx_copy(jnp.minimum(1, last), 1).start()

    per_phase = PEER_SLOTS // (2 * half)

    def step(cur, nxt):
        idx_copy(0, nxt).wait()
        idx_copy(jnp.minimum(i + 2, last), cur).start()
        rows_wait(cur)
        g_tile = g_ref[0]
        for t in range(PEER_GATHER_TOKENS):
            def prefetch(phase):
                for r in range(t * PEER_SLOTS + phase * per_phase, t * PEER_SLOTS + (phase + 1) * per_phase):
                    row_copy(idx_smem[nxt, r], nxt, r // SUBLANES, r % SUBLANES).start(priority=r % 2)
            x_t = x_ref[t:t + 1, :]
            lo = t * groups
            acc = None
            for lg in range(half):
                prefetch(lg)
                term = buf[cur, lo:lo + groups, lg] * x_t[:, lg * LANES:(lg + 1) * LANES]
                acc = term if acc is None else acc + term
            hid = jnp.sum(acc, axis=-1, keepdims=True)
            act = 0.5 * hid * (1.0 + lax.erf(hid * (2.0 ** -0.5)))
            wgt = g_tile[:, t:t + 1].reshape(groups, SUBLANES, 1) * act
            out_rows = []
            for lg in range(half):
                prefetch(half + lg)
                part = jnp.sum(buf[cur, lo:lo + groups, half + lg] * wgt, axis=0)
                out_rows.append(jnp.sum(part, axis=0, keepdims=True))
            z = DEEPNORM_ALPHA * x_t + jnp.concatenate(out_rows, axis=1)
            o_ref[t:t + 1, :] = _layer_norm_rows(z, lng_ref[...], lnb_ref[...])

        @pl.when(i == last)
        def _():
            rows_wait(nxt)
            idx_copy(0, cur).wait()

    @pl.when(i % 2 == 0)
    def _():
        step(0, 1)

    @pl.when(i % 2 == 1)
    def _():
        step(1, 0)


def peer_gather_ln(x, e, g, uv, ln_g, ln_b):
    n = x.shape[0]
    tb = PEER_GATHER_TOKENS
    assert n % tb == 0 and n // tb >= 2
    rows = tb * PEER_SLOTS
    lane_groups = D_MODEL // LANES
    g3 = g.reshape(n // tb, tb, PEER_SLOTS).transpose(0, 2, 1)
    y = pl.pallas_call(
        _peer_gather_kernel,
        grid=(n // tb,),
        in_specs=[pl.BlockSpec(memory_space=pl.ANY),
                  pl.BlockSpec((tb, D_MODEL), lambda i: (i, 0)),
                  pl.BlockSpec((1, PEER_SLOTS, tb), lambda i: (i, 0, 0)),
                  pl.BlockSpec((1, D_MODEL), lambda i: (0, 0)),
                  pl.BlockSpec((1, D_MODEL), lambda i: (0, 0)),
                  pl.BlockSpec(memory_space=pl.ANY)],
        out_specs=pl.BlockSpec((tb, D_MODEL), lambda i: (i, 0)),
        out_shape=jax.ShapeDtypeStruct((n, D_MODEL), jnp.float32),
        scratch_shapes=[pltpu.SMEM((2, rows), jnp.int32),
                        pltpu.VMEM((2, rows // SUBLANES, 2 * lane_groups, SUBLANES, LANES), jnp.float32),
                        pltpu.SemaphoreType.DMA((2,)),
                        pltpu.SemaphoreType.DMA((2,))],
        compiler_params=pltpu.CompilerParams(
            dimension_semantics=("arbitrary",), vmem_limit_bytes=VMEM_LIMIT_BYTES),
        name="peer_gather",
    )(e.reshape(n * PEER_SLOTS), x, g3, ln_g.reshape(1, D_MODEL), ln_b.reshape(1, D_MODEL),
      uv.reshape(uv.shape[0], 2 * lane_groups, LANES))
    return y


def peer_ffn_ln(x, wq_bf16, sk_bf16, uv, ln_g, ln_b):
    B, T, D = x.shape
    xf = x.reshape(B * T, D)
    e_t, g_t = peer_route(xf, wq_bf16, sk_bf16)
    y = peer_gather_ln(xf, e_t.T, g_t.T, uv, ln_g, ln_b)
    return y.reshape(B, T, D)


def _moba_prompt_kernel(qt_ref, k_ref, vt_ref, o_ref, kmean_ref, sel_ref):
    qi = pl.program_id(2)
    nb = k_ref.shape[2]
    blk = k_ref.shape[3]
    scale = HEAD_DIM ** -0.5

    @pl.when(qi == 0)
    def _():
        for nblk in range(nb):
            kmean_ref[nblk:nblk + 1, :] = jnp.sum(k_ref[0, 0, nblk], axis=0, keepdims=True) * (1.0 / blk)

    qt = qt_ref[0, 0, 0].astype(jnp.bfloat16)
    gate = jnp.dot(kmean_ref[...].astype(jnp.bfloat16), qt, preferred_element_type=jnp.float32)
    blk_id = lax.broadcasted_iota(jnp.int32, gate.shape, 0)
    gate = jnp.where(blk_id < qi, gate, NEG_INF)
    rank = jnp.zeros(gate.shape, jnp.int32)
    for m in range(nb):
        gm = gate[m:m + 1, :]
        beats = (gm > gate) | ((gm == gate) & (blk_id > m))
        rank = rank + beats.astype(jnp.int32)
    sel_ref[...] = ((rank < MOBA_TOPK) & (blk_id < qi)).astype(jnp.float32)

    key_pos = lax.broadcasted_iota(jnp.int32, (blk, blk), 0)
    qry_pos = lax.broadcasted_iota(jnp.int32, (blk, blk), 1)
    causal = key_pos <= qry_pos

    def attend_pair(j, carry):
        m_run, l_run, acc = carry
        ss, keeps = [], []
        for d in range(2):
            n = 2 * j + d
            keep = (sel_ref[pl.ds(n, 1), :] > 0.5) | jnp.logical_and(n == qi, causal)
            s = jnp.dot(k_ref[0, 0, n].astype(jnp.bfloat16), qt, preferred_element_type=jnp.float32) * scale
            ss.append(jnp.where(keep, s, NEG_INF))
            keeps.append(keep)
        m_new = jnp.maximum(m_run, jnp.maximum(jnp.max(ss[0], axis=0, keepdims=True),
                                               jnp.max(ss[1], axis=0, keepdims=True)))
        alpha = jnp.exp(m_run - m_new)
        l_new = alpha * l_run
        acc_new = alpha * acc
        for d in range(2):
            p = jnp.where(keeps[d], jnp.exp(ss[d] - m_new), 0.0)
            l_new = l_new + jnp.sum(p, axis=0, keepdims=True)
            acc_new = acc_new + jnp.dot(vt_ref[0, 0, 2 * j + d].astype(jnp.bfloat16), p.astype(jnp.bfloat16),
                                        preferred_element_type=jnp.float32)
        return m_new, l_new, acc_new

    init = (jnp.full((1, blk), NEG_INF, jnp.float32), jnp.zeros((1, blk), jnp.float32),
            jnp.zeros((HEAD_DIM, blk), jnp.float32))
    m_run, l_run, acc = lax.fori_loop(0, (qi + 2) // 2, attend_pair, init)
    o_ref[0, 0, 0] = acc / l_run


def moba_prompt(q, k, v):
    B, T, _ = q.shape
    nb = T // MOBA_BLOCK
    assert T % MOBA_BLOCK == 0 and nb % 2 == 0
    split = lambda t: t.reshape(B, nb, MOBA_BLOCK, MOBA_HEADS, HEAD_DIM)
    qt = split(q).transpose(0, 3, 1, 4, 2)
    kb = split(k).transpose(0, 3, 1, 2, 4)
    vt = split(v).transpose(0, 3, 1, 4, 2)
    ot = pl.pallas_call(
        _moba_prompt_kernel,
        grid=(B, MOBA_HEADS, nb),
        in_specs=[pl.BlockSpec((1, 1, 1, HEAD_DIM, MOBA_BLOCK), lambda b, h, i: (b, h, i, 0, 0)),
                  pl.BlockSpec((1, 1, nb, MOBA_BLOCK, HEAD_DIM), lambda b, h, i: (b, h, 0, 0, 0)),
                  pl.BlockSpec((1, 1, nb, HEAD_DIM, MOBA_BLOCK), lambda b, h, i: (b, h, 0, 0, 0))],
        out_specs=pl.BlockSpec((1, 1, 1, HEAD_DIM, MOBA_BLOCK), lambda b, h, i: (b, h, i, 0, 0)),
        out_shape=jax.ShapeDtypeStruct((B, MOBA_HEADS, nb, HEAD_DIM, MOBA_BLOCK), jnp.float32),
        scratch_shapes=[pltpu.VMEM((nb, HEAD_DIM), jnp.float32),
                        pltpu.VMEM((nb, MOBA_BLOCK), jnp.float32)],
        compiler_params=pltpu.CompilerParams(
            dimension_semantics=("parallel", "parallel", "arbitrary"), vmem_limit_bytes=VMEM_LIMIT_BYTES),
        name="moba_prompt",
    )(qt, kb, vt)
    return ot.transpose(0, 2, 4, 1, 3).reshape(B, T, MOBA_HEADS * HEAD_DIM)


MOBA_GATE_PAGES = 8


def _moba_gate_kernel(pt_ref, q_ref, *refs):
    del pt_ref
    page_refs, sel_ref, gate_ref = refs[:MOBA_GATE_PAGES], refs[MOBA_GATE_PAGES], refs[MOBA_GATE_PAGES + 1]
    p = pl.program_id(1)
    pages_per_block = MOBA_BLOCK // PAGE_SIZE
    blocks_per_step = MOBA_GATE_PAGES // pages_per_block
    lane = lax.broadcasted_iota(jnp.int32, (MOBA_HEADS, LANES), 1)

    @pl.when(p == 0)
    def _():
        gate_ref[...] = jnp.full(gate_ref.shape, NEG_INF, jnp.float32)

    q = q_ref[0].astype(jnp.bfloat16).astype(jnp.float32)
    gate = gate_ref[...]
    for j in range(blocks_per_step):
        block = page_refs[j * pages_per_block][0, 0]
        for h in range(1, pages_per_block):
            block = block + page_refs[j * pages_per_block + h][0, 0]
        k_mean = (jnp.sum(block, axis=-1, keepdims=True) * (1.0 / MOBA_BLOCK)).astype(jnp.bfloat16)
        g = jnp.sum(k_mean.astype(jnp.float32) * q, axis=1)
        gate = jnp.where(lane == p * blocks_per_step + j, g, gate)
    gate_ref[...] = gate

    @pl.when(p == pl.num_programs(1) - 1)
    def _():
        rank = jnp.zeros(gate.shape, jnp.int32)
        for m in range(LANES):
            gm = gate[:, m:m + 1]
            beats = (gm > gate) | ((gm == gate) & (lane > m))
            rank = rank + beats.astype(jnp.int32)
        sel = jnp.zeros(gate.shape, jnp.int32)
        for r in range(MOBA_TOPK):
            idx = jnp.sum(jnp.where(rank == r, lane, 0), axis=1, keepdims=True)
            sel = jnp.where(lane == r, idx, sel)
        sel_ref[0] = sel


def _moba_sample_kernel(n_pages, layer, pt_ref, sel_ref, q_ref, kn_ref, vn_ref, ck_hbm, cv_hbm, o_ref,
                        kbuf, vbuf, sem):
    b = pl.program_id(0)
    pages_per_block = MOBA_BLOCK // PAGE_SIZE
    copies = []
    for h in range(MOBA_HEADS):
        for r in range(MOBA_TOPK):
            blk = sel_ref[(b * MOBA_HEADS + h) * MOBA_TOPK + r]
            for half in range(pages_per_block):
                page = pt_ref[b * n_pages + blk * pages_per_block + half]
                dst = pl.ds((r * pages_per_block + half) * PAGE_SIZE, PAGE_SIZE)
                copies.append(pltpu.make_async_copy(ck_hbm.at[layer, page, h], kbuf.at[h, :, dst], sem.at[0]))
                copies.append(pltpu.make_async_copy(cv_hbm.at[layer, page, h], vbuf.at[h, :, dst], sem.at[1]))
    for c in copies:
        c.start()
    for c in copies:
        c.wait()
    scale = HEAD_DIM ** -0.5
    nt = (((1,), (1,)), ((), ()))
    for h in range(MOBA_HEADS):
        q = q_ref[0, h:h + 1, :].astype(jnp.bfloat16)
        kn = kn_ref[0, h:h + 1, :].astype(jnp.bfloat16)
        vn = vn_ref[0, h:h + 1, :].astype(jnp.bfloat16)
        s_sel = jnp.dot(q, kbuf[h].astype(jnp.bfloat16), preferred_element_type=jnp.float32) * scale
        s_own = jnp.sum(q.astype(jnp.float32) * kn.astype(jnp.float32), axis=1, keepdims=True) * scale
        m = jnp.maximum(jnp.max(s_sel, axis=1, keepdims=True), s_own)
        p_sel = jnp.exp(s_sel - m)
        p_own = jnp.exp(s_own - m)
        denom = jnp.sum(p_sel, axis=1, keepdims=True) + p_own
        p_sel = (p_sel / denom).astype(jnp.bfloat16)
        p_own = (p_own / denom).astype(jnp.bfloat16).astype(jnp.float32)
        o = lax.dot_general(p_sel, vbuf[h].astype(jnp.bfloat16), nt, preferred_element_type=jnp.float32)
        o_ref[0, h:h + 1, :] = o + p_own * vn.astype(jnp.float32)


def moba_sample(q, k_new, v_new, cache_k, cache_v, layer, page_table):
    B, n_pages = page_table.shape
    assert n_pages % MOBA_GATE_PAGES == 0 and MOBA_GATE_PAGES % (MOBA_BLOCK // PAGE_SIZE) == 0
    assert MOBA_TOPK <= n_pages * PAGE_SIZE // MOBA_BLOCK <= LANES
    pt = page_table.reshape(B * n_pages)
    ck = cache_k.transpose(0, 1, 3, 4, 2)
    cv = cache_v.transpose(0, 1, 3, 4, 2)
    steps = n_pages // MOBA_GATE_PAGES

    def page_spec(j):
        return pl.BlockSpec((1, 1, MOBA_HEADS, HEAD_DIM, PAGE_SIZE),
                            lambda b, p, pt_ref: (layer, pt_ref[b * n_pages + p * MOBA_GATE_PAGES + j], 0, 0, 0))

    sel = pl.pallas_call(
        _moba_gate_kernel,
        grid_spec=pltpu.PrefetchScalarGridSpec(
            num_scalar_prefetch=1,
            grid=(B, steps),
            in_specs=[pl.BlockSpec((1, MOBA_HEADS, HEAD_DIM, 1), lambda b, p, pt_ref: (b, 0, 0, 0))]
                     + [page_spec(j) for j in range(MOBA_GATE_PAGES)],
            out_specs=pl.BlockSpec((1, MOBA_HEADS, LANES), lambda b, p, pt_ref: (b, 0, 0)),
            scratch_shapes=[pltpu.VMEM((MOBA_HEADS, LANES), jnp.float32)]),
        out_shape=jax.ShapeDtypeStruct((B, MOBA_HEADS, LANES), jnp.int32),
        compiler_params=pltpu.CompilerParams(
            dimension_semantics=("parallel", "arbitrary"), vmem_limit_bytes=VMEM_LIMIT_BYTES),
        name="moba_gate",
    )(pt, q.reshape(B, MOBA_HEADS, HEAD_DIM, 1), *([ck] * MOBA_GATE_PAGES))
    sel_flat = sel[:, :, :MOBA_TOPK].reshape(B * MOBA_HEADS * MOBA_TOPK)
    keys = MOBA_TOPK * MOBA_BLOCK
    vec = pl.BlockSpec((1, MOBA_HEADS, HEAD_DIM), lambda b, *_: (b, 0, 0))
    return pl.pallas_call(
        functools.partial(_moba_sample_kernel, n_pages, layer),
        grid_spec=pltpu.PrefetchScalarGridSpec(
            num_scalar_prefetch=2,
            grid=(B,),
            in_specs=[vec, vec, vec, pl.BlockSpec(memory_space=pl.ANY), pl.BlockSpec(memory_space=pl.ANY)],
            out_specs=vec,
            scratch_shapes=[pltpu.VMEM((MOBA_HEADS, HEAD_DIM, keys), jnp.float32),
                            pltpu.VMEM((MOBA_HEADS, HEAD_DIM, keys), jnp.float32),
                            pltpu.SemaphoreType.DMA((2,))]),
        out_shape=jax.ShapeDtypeStruct((B, MOBA_HEADS, HEAD_DIM), jnp.float32),
        compiler_params=pltpu.CompilerParams(
            dimension_semantics=("arbitrary",), vmem_limit_bytes=VMEM_LIMIT_BYTES),
        name="moba_sample",
    )(pt, sel_flat, q, k_new, v_new, ck, cv)


def _layer_norm_rows(z, g, b):
    mu = jnp.mean(z, axis=-1, keepdims=True)
    zc = z - mu
    var = jnp.mean(zc * zc, axis=-1, keepdims=True)
    return zc * lax.rsqrt(var + LN_EPS) * g + b


def _proj_res_ln_kernel(a_ref, w_ref, x_ref, g_ref, b_ref, o_ref):
    y = jnp.dot(a_ref[...].astype(jnp.bfloat16), w_ref[...], preferred_element_type=jnp.float32)
    o_ref[...] = _layer_norm_rows(DEEPNORM_ALPHA * x_ref[...] + y, g_ref[...], b_ref[...])


def proj_res_ln(a, w_bf16, x, ln_g, ln_b, tm=512):
    m = a.shape[0]
    tm = min(tm, m)
    assert m % tm == 0
    row = pl.BlockSpec((tm, D_MODEL), lambda i: (i, 0))
    vec = pl.BlockSpec((1, D_MODEL), lambda i: (0, 0))
    return pl.pallas_call(
        _proj_res_ln_kernel,
        grid=(m // tm,),
        in_specs=[row, pl.BlockSpec((D_MODEL, D_MODEL), lambda i: (0, 0)), row, vec, vec],
        out_specs=row,
        out_shape=jax.ShapeDtypeStruct((m, D_MODEL), jnp.float32),
        compiler_params=pltpu.CompilerParams(
            dimension_semantics=("parallel",), vmem_limit_bytes=VMEM_LIMIT_BYTES),
        name="proj_res_ln",
    )(a, w_bf16, x, ln_g.reshape(1, D_MODEL), ln_b.reshape(1, D_MODEL))


def _mem_attn_kernel(x_ref, wq_ref, wo_ref, mk_ref, mv_ref, g_ref, b_ref, o_ref):
    x = x_ref[0]
    q = jnp.dot(x.astype(jnp.bfloat16), wq_ref[...], preferred_element_type=jnp.float32)
    nt = (((1,), (1,)), ((), ()))
    heads = []
    for h in range(MEM_HEADS):
        sl = slice(h * MEM_HEAD_DIM, (h + 1) * MEM_HEAD_DIM)
        s = lax.dot_general(q[:, sl].astype(jnp.bfloat16), mk_ref[0, :, sl].astype(jnp.bfloat16), nt,
                            preferred_element_type=jnp.float32) * MEM_HEAD_DIM ** -0.5
        p = jnp.exp(s - jnp.max(s, axis=-1, keepdims=True))
        p = p / jnp.sum(p, axis=-1, keepdims=True)
        heads.append(jnp.dot(p.astype(jnp.bfloat16), mv_ref[0, :, sl].astype(jnp.bfloat16),
                             preferred_element_type=jnp.float32))
    o = jnp.concatenate(heads, axis=1)
    cross = jnp.dot(o.astype(jnp.bfloat16), wo_ref[...], preferred_element_type=jnp.float32)
    o_ref[0] = _layer_norm_rows(DEEPNORM_ALPHA * x + cross, g_ref[...], b_ref[...])


def mem_attend_ln(x, mk, mv, wq_bf16, wo_bf16, ln_g, ln_b, tq=512):
    B, T, _ = x.shape
    mem_len = mk.shape[1]
    tq = min(tq, T)
    assert T % tq == 0
    xspec = pl.BlockSpec((1, tq, D_MODEL), lambda b, t: (b, t, 0))
    wspec = pl.BlockSpec((D_MODEL, D_MODEL), lambda b, t: (0, 0))
    mspec = pl.BlockSpec((1, mem_len, D_MODEL), lambda b, t: (b, 0, 0))
    vec = pl.BlockSpec((1, D_MODEL), lambda b, t: (0, 0))
    return pl.pallas_call(
        _mem_attn_kernel,
        grid=(B, T // tq),
        in_specs=[xspec, wspec, wspec, mspec, mspec, vec, vec],
        out_specs=xspec,
        out_shape=jax.ShapeDtypeStruct((B, T, D_MODEL), jnp.float32),
        compiler_params=pltpu.CompilerParams(
            dimension_semantics=("parallel", "parallel"), vmem_limit_bytes=VMEM_LIMIT_BYTES),
        name="mem_attn",
    )(x, wq_bf16, wo_bf16, mk.reshape(B, mem_len, D_MODEL), mv.reshape(B, mem_len, D_MODEL),
      ln_g.reshape(1, D_MODEL), ln_b.reshape(1, D_MODEL))


def rope(x, pos):
    half = x.shape[-1] // 2
    inv = ROPE_THETA ** (-jnp.arange(half, dtype=jnp.float32) / half)
    ang = pos.astype(jnp.float32)[:, None] * inv[None, :]
    cos = jnp.cos(ang)[None, :, None, :]
    sin = jnp.sin(ang)[None, :, None, :]
    x1 = x[..., :half].astype(jnp.float32)
    x2 = x[..., half:].astype(jnp.float32)
    return jnp.concatenate([x1 * cos - x2 * sin, x2 * cos + x1 * sin], -1).astype(x.dtype)


WKV_TIME_CHUNK = 32


def _wkv_scan_kernel(w_ref, nkk_ref, kka_ref, k_ref, r_ref, v_ref, s0_ref, y_ref, sT_ref, s_ref):
    tc = pl.program_id(1)
    steps = w_ref.shape[0]
    nvi = s_ref.shape[0]

    @pl.when(tc == 0)
    def _():
        s_ref[...] = s0_ref[...]

    def step(t, carry):
        w = w_ref[t]
        nkk = nkk_ref[t]
        kka = kka_ref[t]
        kv = k_ref[t]
        r = r_ref[t]
        for vi in range(nvi):
            s = s_ref[vi]
            sa = jnp.sum(s * nkk, axis=0, keepdims=True)
            s_new = s * w + sa * kka + v_ref[t, vi:vi + 1, :] * kv
            s_ref[vi] = s_new
            y_ref[t, vi:vi + 1, :] = jnp.sum(s_new * r, axis=0, keepdims=True)
        return carry

    lax.fori_loop(0, steps, step, 0)

    @pl.when(tc == pl.num_programs(1) - 1)
    def _():
        sT_ref[...] = s_ref[...]


def wkv_scan_pallas(r, decay, k, v, kk, a, s0):
    B, T, H, N = r.shape
    bh = B * H
    vh = max(1, LANES // bh)
    nvi = N // vh
    width = vh * bh
    assert width % LANES == 0
    tcs = min(WKV_TIME_CHUNK, T)
    assert T % tcs == 0

    def key_major(x):
        xt = jnp.transpose(x, (1, 3, 0, 2)).reshape(T, N, bh)
        return jnp.concatenate([xt] * vh, axis=-1)

    v_l = jnp.transpose(v.reshape(B, T, H, vh, nvi), (1, 4, 3, 0, 2)).reshape(T, nvi, width)
    s0_l = jnp.transpose(s0.reshape(B, H, vh, nvi, N), (3, 4, 2, 0, 1)).reshape(nvi, N, width)
    kspec = pl.BlockSpec((tcs, N, LANES), lambda j, t: (t, 0, j))
    vspec = pl.BlockSpec((tcs, nvi, LANES), lambda j, t: (t, 0, j))
    sspec = pl.BlockSpec((nvi, N, LANES), lambda j, t: (0, 0, j))
    y_l, sT_l = pl.pallas_call(
        _wkv_scan_kernel,
        grid=(width // LANES, T // tcs),
        in_specs=[kspec, kspec, kspec, kspec, kspec, vspec, sspec],
        out_specs=[vspec, sspec],
        out_shape=[jax.ShapeDtypeStruct((T, nvi, width), jnp.float32),
                   jax.ShapeDtypeStruct((nvi, N, width), jnp.float32)],
        scratch_shapes=[pltpu.VMEM((nvi, N, LANES), jnp.float32)],
        compiler_params=pltpu.CompilerParams(
            dimension_semantics=("parallel", "arbitrary"), vmem_limit_bytes=VMEM_LIMIT_BYTES),
        name="wkv_scan",
    )(key_major(decay), key_major(-kk), key_major(kk * a), key_major(k), key_major(r), v_l, s0_l)
    y = jnp.transpose(y_l.reshape(T, nvi, vh, B, H), (3, 0, 4, 2, 1)).reshape(B, T, H, N)
    sT = jnp.transpose(sT_l.reshape(nvi, N, vh, B, H), (3, 4, 2, 0, 1)).reshape(B, H, N, N)
    return y, sT


def rwkv_group(proj, prev, s0, mu, w0, w2, a0, a2, g2, k_k, k_a, r_k, lnx_g, lnx_b):
    B, T, _ = proj.shape
    shifted = jnp.concatenate([prev[:, None, :].astype(proj.dtype), proj[:, :-1]], axis=1)
    xm = proj + mu * (shifted - proj)
    c1, c2, c3 = RWKV_WIDTH, 2 * RWKV_WIDTH, 3 * RWKV_WIDTH
    c4 = c3 + LORA_W
    c5 = c4 + LORA_A
    r, k, v = xm[..., :c1], xm[..., c1:c2], xm[..., c2:c3]
    xw, xa, xg = xm[..., c3:c4], xm[..., c4:c5], xm[..., c5:]
    w = -jax.nn.softplus(-(w0 + jnp.tanh(xw) @ w2)) - 0.5
    decay = jnp.exp(-jnp.exp(w.astype(jnp.float32)))
    a = jax.nn.sigmoid(a0 + xa @ a2)
    g = jax.nn.sigmoid(xg) @ g2
    heads = lambda t: t.astype(jnp.float32).reshape(B, T, RWKV_HEADS, HEAD_DIM)
    kk = heads(k * k_k)
    kk = kk / jnp.maximum(jnp.sqrt(jnp.sum(kk * kk, -1, keepdims=True)), 1e-12)
    k = k * (1.0 + (a - 1.0) * k_a)
    r_h, k_h, v_h, a_h, w_h = heads(r), heads(k), heads(v), heads(a), heads(decay)
    y, s = wkv_scan_pallas(r_h, w_h, k_h, v_h, kk, a_h, s0.astype(jnp.float32))
    mean = jnp.mean(y, -1, keepdims=True)
    var = jnp.mean(jnp.square(y - mean), -1, keepdims=True)
    y = ((y - mean) * lax.rsqrt(var + GN_EPS)).reshape(B, T, RWKV_WIDTH) * lnx_g + lnx_b
    bonus = (jnp.sum(r_h * k_h * r_k, -1, keepdims=True) * v_h).reshape(B, T, RWKV_WIDTH)
    out = ((y + bonus) * g).astype(proj.dtype)
    return out, s.astype(s0.dtype), proj[:, -1]


def run_trunk(x, pos0, shift0, wkv0, cache_k, cache_v, page_table, mem_k, mem_v, p):
    B, T, _ = x.shape
    pos = pos0 + jnp.arange(T, dtype=jnp.int32)
    k_rows, v_rows, wkvs, shifts = [], [], [], []
    for i in range(DEPTH):
        proj_rw, q, k, v = in_proj(x.reshape(B * T, D_MODEL), p['w_in_bf16'][i])
        rw_out, wkv_i, shift_i = rwkv_group(
            proj_rw.reshape(B, T, RWKV_PROJ), shift0[i], wkv0[i], p['shift_mu'][i],
            p['rwkv_w0'][i], p['rwkv_w2'][i], p['rwkv_a0'][i], p['rwkv_a2'][i],
            p['rwkv_g2'][i], p['rwkv_k_k'][i], p['rwkv_k_a'][i], p['rwkv_r_k'][i],
            p['rwkv_lnx_g'][i], p['rwkv_lnx_b'][i])
        q = rope(q.reshape(B, T, MOBA_HEADS, HEAD_DIM), pos)
        k = rope(k.reshape(B, T, MOBA_HEADS, HEAD_DIM), pos)
        v = v.reshape(B, T, MOBA_HEADS, HEAD_DIM)
        if cache_k is None:
            mb_out = moba_prompt(q.reshape(B, T, MOBA_WIDTH), k.reshape(B, T, MOBA_WIDTH),
                                 v.reshape(B, T, MOBA_WIDTH))
        else:
            assert T == 1 and pos0 % MOBA_BLOCK == 0
            mb_out = moba_sample(q[:, 0], k[:, 0], v[:, 0], cache_k, cache_v, i,
                                 page_table).reshape(B, T, MOBA_WIDTH)
        x = proj_res_ln(jnp.concatenate([rw_out, mb_out], axis=-1).reshape(B * T, D_MODEL), p['w_out_bf16'][i],
                        x.reshape(B * T, D_MODEL), p['ln1_g'][i], p['ln1_b'][i]).reshape(B, T, D_MODEL)
        x = mem_attend_ln(x, mem_k[i], mem_v[i], p['w_mem_q_bf16'][i], p['w_mem_o_bf16'][i],
                          p['ln2_g'][i], p['ln2_b'][i])
        x = peer_ffn_ln(x, p['peer_w_q_bf16'][i], p['peer_sub_keys_bf16'][i], p['peer_uv'][i],
                        p['ln3_g'][i], p['ln3_b'][i])
        k_rows.append(k)
        v_rows.append(v)
        wkvs.append(wkv_i)
        shifts.append(shift_i)
    return x, jnp.stack(k_rows), jnp.stack(v_rows), jnp.stack(wkvs), jnp.stack(shifts)


def kernel(x_prompt, x_sample, mem_prompt, cache_k, cache_v, page_table, state_wkv,
           state_shift, cache_mem_k, cache_mem_v, w_in, shift_mu, rwkv_w0, rwkv_w2,
           rwkv_a0, rwkv_a2, rwkv_g2, rwkv_k_k, rwkv_k_a, rwkv_r_k, rwkv_lnx_g,
           rwkv_lnx_b, w_out, ln1_g, ln1_b, w_mem_q, w_mem_kv, w_mem_o, ln2_g, ln2_b,
           peer_w_q, peer_sub_keys, peer_u, peer_v, ln3_g, ln3_b):
    p = dict(w_in=w_in, shift_mu=shift_mu, rwkv_w0=rwkv_w0, rwkv_w2=rwkv_w2,
             rwkv_a0=rwkv_a0, rwkv_a2=rwkv_a2, rwkv_g2=rwkv_g2, rwkv_k_k=rwkv_k_k,
             rwkv_k_a=rwkv_k_a, rwkv_r_k=rwkv_r_k, rwkv_lnx_g=rwkv_lnx_g,
             rwkv_lnx_b=rwkv_lnx_b, w_out=w_out, ln1_g=ln1_g, ln1_b=ln1_b,
             w_mem_q=w_mem_q, w_mem_o=w_mem_o, ln2_g=ln2_g, ln2_b=ln2_b,
             ln3_g=ln3_g, ln3_b=ln3_b)
    p['peer_uv'] = jnp.concatenate([peer_u, peer_v], axis=-1)
    p['peer_w_q_bf16'] = peer_w_q.astype(jnp.bfloat16)
    p['peer_sub_keys_bf16'] = peer_sub_keys.astype(jnp.bfloat16)
    p['w_in_bf16'] = w_in.astype(jnp.bfloat16)
    p['w_out_bf16'] = w_out.astype(jnp.bfloat16)
    p['w_mem_q_bf16'] = w_mem_q.astype(jnp.bfloat16)
    p['w_mem_o_bf16'] = w_mem_o.astype(jnp.bfloat16)
    B = x_prompt.shape[0]
    mem_len = mem_prompt.shape[1]
    mkv = jnp.stack([pmatmul(mem_prompt.reshape(B * mem_len, D_MODEL), w_mem_kv[i]).reshape(B, mem_len, 2 * D_MODEL)
                     for i in range(DEPTH)])
    mem_k_prompt = mkv[..., :D_MODEL].reshape(DEPTH, B, mem_len, MEM_HEADS, MEM_HEAD_DIM)
    mem_v_prompt = mkv[..., D_MODEL:].reshape(DEPTH, B, mem_len, MEM_HEADS, MEM_HEAD_DIM)
    shift0 = jnp.zeros((DEPTH, B, RWKV_PROJ), x_prompt.dtype)
    wkv0 = jnp.zeros((DEPTH, B, RWKV_HEADS, HEAD_DIM, HEAD_DIM), x_prompt.dtype)
    past_len = page_table.shape[1] * PAGE_SIZE
    y_sample, k_sample, v_sample, wkv_sample, shift_sample = run_trunk(
        x_sample, past_len, state_shift, state_wkv, cache_k, cache_v, page_table,
        cache_mem_k, cache_mem_v, p)
    y_prompt, k_prompt, v_prompt, wkv_prompt, shift_prompt = run_trunk(
        x_prompt, 0, shift0, wkv0, None, None, None, mem_k_prompt, mem_v_prompt, p)
    return (y_prompt, y_sample, k_prompt, v_prompt, wkv_prompt, shift_prompt,
            mem_k_prompt, mem_v_prompt, k_sample, v_sample, wkv_sample, shift_sample)
```
